```python
import math
import jax, jax.numpy as jnp
from jax import lax
import numpy as np

D_MODEL = 1024
BATCH = 16
SEQ = 4096
DEPTH = 2
DEC_BATCH = 32
DEC_SEQ = 32
PAST_LEN = 2048

CHUNK = 64
D_CONV = 512
CONV_W = 31
H_RET = 8
DK_RET = 64
DV_RET = 64
D_RET = H_RET * DK_RET
H_FOX = 8
DH_FOX = 64
D_FOX = H_FOX * DH_FOX
D_FF = 2816
FFN_CONV_W = 3
Q_BLOCK = 128
ROPE_BASE = 10000.0
EPS = 1e-6
N_BRANCH = 3
SEG_SIZES = (D_CONV, D_CONV, D_RET, D_RET, H_RET * DV_RET, H_RET * DV_RET, D_FOX, D_FOX, D_FOX, H_FOX, N_BRANCH * D_MODEL)
D_IN = sum(SEG_SIZES)

kernel_name = 'hybrid_streaming_conv_retention_fox_step'


def split_in(proj):
    idx = []
    acc = 0
    for s in SEG_SIZES[:-1]:
        acc += s
        idx.append(acc)
    return jnp.split(proj, idx, axis=-1)


def rms_norm(x, g):
    xf = x.astype(jnp.float32)
    y = xf * lax.rsqrt(jnp.mean(xf * xf, axis=-1, keepdims=True) + EPS)
    return (y * g.astype(jnp.float32)).astype(x.dtype)


def layer_norm(x, g, b):
    xf = x.astype(jnp.float32)
    mu = jnp.mean(xf, axis=-1, keepdims=True)
    xc = xf - mu
    var = jnp.mean(xc * xc, axis=-1, keepdims=True)
    return (xc * lax.rsqrt(var + EPS) * g.astype(jnp.float32) + b.astype(jnp.float32)).astype(x.dtype)


def dwconv(xp, w):
    c = xp.shape[-1]
    return lax.conv_general_dilated(xp, w[:, None, :].astype(xp.dtype), window_strides=(1,), padding='VALID',
                                    dimension_numbers=('NWC', 'WIO', 'NWC'), feature_group_count=c)


def rotary(x, pos):
    half = x.shape[-1] // 2
    inv = jnp.exp(-math.log(ROPE_BASE) * jnp.arange(half, dtype=jnp.float32) / half)
    ang = pos.astype(jnp.float32)[:, None] * inv[None, :]
    cos = jnp.cos(ang)[None, :, None, :]
    sin = jnp.sin(ang)[None, :, None, :]
    xf = x.astype(jnp.float32)
    x1, x2 = xf[..., :half], xf[..., half:]
    return jnp.concatenate([x1 * cos - x2 * sin, x1 * sin + x2 * cos], axis=-1)


def retention_chunks(q, k, v, state0):
    c = q.shape[2]
    lg = jnp.log1p(-jnp.exp2(-5.0 - jnp.arange(H_RET, dtype=jnp.float32)))
    idx = jnp.arange(c, dtype=jnp.float32)
    intra = jnp.exp(lg[:, None, None] * jnp.abs(idx[:, None] - idx[None, :]))
    s = jnp.einsum('bnihd,bnjhd->bnhij', q, k) * intra
    o = jnp.einsum('bnhij,bnjhv->bnihv', s, v)
    k_end = k * jnp.exp(lg[None, :] * (c - 1.0 - idx)[:, None])[:, :, None]
    kv = jnp.einsum('bnjhd,bnjhv->nbhdv', k_end, v)
    decay_c = jnp.exp(lg * c)[None, :, None, None]

    def step(st, kv_n):
        return st * decay_c + kv_n, st

    s_final, s_prev = lax.scan(step, state0, kv)
    q_in = q * jnp.exp(lg[None, :] * (idx + 1.0)[:, None])[:, :, None]
    o = o + jnp.einsum('bnihd,nbhdv->bnihv', q_in, s_prev)
    return o, s_final


def fox_block(q, c_q, pos_q, k, v, c_k, pos_k):
    s = jnp.einsum('bthd,bshd->bhts', q, k).astype(jnp.float32) * (DH_FOX ** -0.5)
    bias = jnp.transpose(c_q, (0, 2, 1))[..., :, None] - jnp.transpose(c_k, (0, 2, 1))[..., None, :]
    mask = pos_k[None, :] <= pos_q[:, None]
    s = jnp.where(mask[None, None], s + bias, -jnp.inf)
    p = jax.nn.softmax(s, axis=-1)
    return jnp.einsum('bhts,bshd->bthd', p.astype(v.dtype), v)


def fox_prompt(q, k, v, c, pos):
    b, s, h, d = q.shape
    nb = s // Q_BLOCK
    qb = jnp.moveaxis(q.reshape(b, nb, Q_BLOCK, h, d), 1, 0)
    cb = jnp.moveaxis(c.reshape(b, nb, Q_BLOCK, h), 1, 0)
    pb = pos.reshape(nb, Q_BLOCK)
    ob = lax.map(lambda a: fox_block(a[0], a[1], a[2], k, v, c, pos), (qb, cb, pb))
    return jnp.moveaxis(ob, 0, 1).reshape(b, s, h, d)


def trunk_layer(x, pos, conv_buf, ret_state, fox_past, ffn_buf,
                g_mix, w_in, b_fgate, w_dw_conv, b_dw_conv, ln_conv_g, ln_conv_b, w_conv_out,
                gn_ret_g, w_ret_out, g_q_fox, g_k_fox, w_fox_out, w_out,
                g_ffn, w_ffn_up, w_ffn_dw, b_ffn_dw, w_ffn_down):
    b, t, _ = x.shape
    h = rms_norm(x, g_mix)
    glu_a, glu_b, rq, rk, rv, rg, fq, fk, fv, ff, gates = split_in(h @ w_in)

    u = glu_a * jax.nn.sigmoid(glu_b)
    up = jnp.concatenate([conv_buf.astype(u.dtype), u], axis=1)
    cv = dwconv(up, w_dw_conv) + b_dw_conv
    y_conv = jax.nn.silu(layer_norm(cv, ln_conv_g, ln_conv_b)) @ w_conv_out
    new_conv_buf = up[:, -(CONV_W - 1):]

    q = rotary(rq.reshape(b, t, H_RET, DK_RET), pos)
    k = rotary(rk.reshape(b, t, H_RET, DK_RET), pos) * (DK_RET ** -0.5)
    v = rv.reshape(b, t, H_RET, DV_RET).astype(jnp.float32)
    c = min(t, CHUNK)
    n = t // c
    o, new_ret = retention_chunks(q.reshape(b, n, c, H_RET, DK_RET), k.reshape(b, n, c, H_RET, DK_RET),
                                  v.reshape(b, n, c, H_RET, DV_RET), ret_state.astype(jnp.float32))
    o = o.reshape(b, t, H_RET, DV_RET)
    mu = jnp.mean(o, axis=-1, keepdims=True)
    oc = o - mu
    o = oc * lax.rsqrt(jnp.mean(oc * oc, axis=-1, keepdims=True) + EPS)
    o = (o.reshape(b, t, H_RET * DV_RET) * gn_ret_g.astype(jnp.float32)).astype(x.dtype)
    y_ret = (jax.nn.silu(rg) * o) @ w_ret_out

    fq = rms_norm(fq.reshape(b, t, H_FOX, DH_FOX), g_q_fox)
    fk = rms_norm(fk.reshape(b, t, H_FOX, DH_FOX), g_k_fox)
    fv = fv.reshape(b, t, H_FOX, DH_FOX)
    logf = jax.nn.log_sigmoid((ff + b_fgate).astype(jnp.float32))
    if fox_past is None:
        cum = jnp.cumsum(logf, axis=1)
        of = fox_prompt(fq, fk, fv, cum, pos)
    else:
        k_past, v_past, logf_past = fox_past
        p_len = k_past.shape[1]
        k_all = jnp.concatenate([k_past.astype(fk.dtype), fk], axis=1)
        v_all = jnp.concatenate([v_past.astype(fv.dtype), fv], axis=1)
        cum = jnp.cumsum(jnp.concatenate([logf_past.astype(jnp.float32), logf], axis=1), axis=1)
        pos_k = jnp.arange(p_len + t)
        of = fox_block(fq, cum[:, p_len:], pos, k_all, v_all, cum, pos_k)
    y_fox = of.reshape(b, t, D_FOX) @ w_fox_out

    g = jax.nn.sigmoid(gates.reshape(b, t, N_BRANCH, D_MODEL))
    merged = g[:, :, 0] * y_conv + g[:, :, 1] * y_ret + g[:, :, 2] * y_fox
    x = x + merged @ w_out

    h2 = rms_norm(x, g_ffn)
    a, gb = jnp.split(h2 @ w_ffn_up, 2, axis=-1)
    ap = jnp.concatenate([ffn_buf.astype(a.dtype), a], axis=1)
    a = dwconv(ap, w_ffn_dw) + b_ffn_dw
    x = x + (jax.nn.gelu(a, approximate=False) * gb) @ w_ffn_down
    new_ffn_buf = ap[:, -(FFN_CONV_W - 1):]

    return x, (new_conv_buf, new_ret.astype(x.dtype), fk, fv, logf.astype(x.dtype), new_ffn_buf)


def setup_inputs(seed: int = 0) -> dict:
    key = jax.random.key(seed)
    ks = jax.random.split(key, 32)
    f32 = jnp.float32

    def nrm(k, shape, scale):
        return jax.random.normal(k, shape, f32) * scale

    def gain(k, shape):
        return 1.0 + 0.05 * jax.random.normal(k, shape, f32)

    return {
        'x_prompt': nrm(ks[0], (BATCH, SEQ, D_MODEL), 1.0),
        'x_sample': nrm(ks[1], (DEC_BATCH, DEC_SEQ, D_MODEL), 1.0),
        'state_conv': nrm(ks[2], (DEPTH, DEC_BATCH, CONV_W - 1, D_CONV), 0.5),
        'state_ret': nrm(ks[3], (DEPTH, DEC_BATCH, H_RET, DK_RET, DV_RET), 0.5),
        'cache_fox_k': nrm(ks[4], (DEPTH, DEC_BATCH, PAST_LEN, H_FOX, DH_FOX), 1.0),
        'cache_fox_v': nrm(ks[5], (DEPTH, DEC_BATCH, PAST_LEN, H_FOX, DH_FOX), 1.0),
        'cache_fox_logf': jax.nn.log_sigmoid(3.0 + jax.random.normal(ks[6], (DEPTH, DEC_BATCH, PAST_LEN, H_FOX), f32)),
        'state_ffn_conv': nrm(ks[7], (DEPTH, DEC_BATCH, FFN_CONV_W - 1, D_FF), 0.5),
        'g_mix': gain(ks[8], (DEPTH, D_MODEL)),
        'w_in': nrm(ks[9], (DEPTH, D_MODEL, D_IN), D_MODEL ** -0.5),
        'b_fgate': jnp.linspace(1.0, 5.0, H_FOX, dtype=f32)[None, :] + nrm(ks[10], (DEPTH, H_FOX), 0.1),
        'w_dw_conv': nrm(ks[11], (DEPTH, CONV_W, D_CONV), CONV_W ** -0.5),
        'b_dw_conv': nrm(ks[12], (DEPTH, D_CONV), 0.02),
        'ln_conv_g': gain(ks[13], (DEPTH, D_CONV)),
        'ln_conv_b': nrm(ks[14], (DEPTH, D_CONV), 0.02),
        'w_conv_out': nrm(ks[15], (DEPTH, D_CONV, D_MODEL), D_CONV ** -0.5),
        'gn_ret_g': gain(ks[16], (DEPTH, H_RET * DV_RET)),
        'w_ret_out': nrm(ks[17], (DEPTH, H_RET * DV_RET, D_MODEL), (H_RET * DV_RET) ** -0.5),
        'g_q_fox': gain(ks[18], (DEPTH, DH_FOX)),
        'g_k_fox': gain(ks[19], (DEPTH, DH_FOX)),
        'w_fox_out': nrm(ks[20], (DEPTH, D_FOX, D_MODEL), D_FOX ** -0.5),
        'w_out': nrm(ks[21], (DEPTH, D_MODEL, D_MODEL), D_MODEL ** -0.5),
        'g_ffn': gain(ks[22], (DEPTH, D_MODEL)),
        'w_ffn_up': nrm(ks[23], (DEPTH, D_MODEL, 2 * D_FF), D_MODEL ** -0.5),
        'w_ffn_dw': nrm(ks[24], (DEPTH, FFN_CONV_W, D_FF), FFN_CONV_W ** -0.5),
        'b_ffn_dw': nrm(ks[25], (DEPTH, D_FF), 0.02),
        'w_ffn_down': nrm(ks[26], (DEPTH, D_FF, D_MODEL), D_FF ** -0.5),
    }


def reference(x_prompt, x_sample, state_conv, state_ret, cache_fox_k, cache_fox_v, cache_fox_logf, state_ffn_conv,
              g_mix, w_in, b_fgate, w_dw_conv, b_dw_conv, ln_conv_g, ln_conv_b, w_conv_out,
              gn_ret_g, w_ret_out, g_q_fox, g_k_fox, w_fox_out, w_out,
              g_ffn, w_ffn_up, w_ffn_dw, b_ffn_dw, w_ffn_down):
    bp, sp, _ = x_prompt.shape
    p_len = cache_fox_k.shape[2]
    ts = x_sample.shape[1]
    pos_p = jnp.arange(sp)
    pos_s = p_len + jnp.arange(ts)
    xp, xs = x_prompt, x_sample
    pc, pr, pk, pv, pf, pn = [], [], [], [], [], []
    sc, sr, sk, sv, sf, sn = [], [], [], [], [], []
    for l in range(DEPTH):
        lw = (g_mix[l], w_in[l], b_fgate[l], w_dw_conv[l], b_dw_conv[l], ln_conv_g[l], ln_conv_b[l], w_conv_out[l],
              gn_ret_g[l], w_ret_out[l], g_q_fox[l], g_k_fox[l], w_fox_out[l], w_out[l],
              g_ffn[l], w_ffn_up[l], w_ffn_dw[l], b_ffn_dw[l], w_ffn_down[l])
        xp, st_p = trunk_layer(xp, pos_p,
                               jnp.zeros((bp, CONV_W - 1, D_CONV), xp.dtype),
                               jnp.zeros((bp, H_RET, DK_RET, DV_RET), jnp.float32),
                               None,
                               jnp.zeros((bp, FFN_CONV_W - 1, D_FF), xp.dtype), *lw)
        xs, st_s = trunk_layer(xs, pos_s, state_conv[l], state_ret[l],
                               (cache_fox_k[l], cache_fox_v[l], cache_fox_logf[l]),
                               state_ffn_conv[l], *lw)
        pc.append(st_p[0]); pr.append(st_p[1]); pk.append(st_p[2]); pv.append(st_p[3]); pf.append(st_p[4]); pn.append(st_p[5])
        sc.append(st_s[0]); sr.append(st_s[1]); sk.append(st_s[2]); sv.append(st_s[3]); sf.append(st_s[4]); sn.append(st_s[5])
    return (xp, xs,
            jnp.stack(pc), jnp.stack(pr), jnp.stack(pk), jnp.stack(pv), jnp.stack(pf), jnp.stack(pn),
            jnp.stack(sc), jnp.stack(sr), jnp.stack(sk), jnp.stack(sv), jnp.stack(sf), jnp.stack(sn))
```

```python
import functools
import math

import jax
import jax.numpy as jnp
import numpy as np
from jax import lax
from jax.experimental import pallas as pl
from jax.experimental.pallas import tpu as pltpu

F32 = jnp.float32
BF16 = jnp.bfloat16

EPS = 1e-6
ROPE_BASE = 10000.0
CHUNK = 64
N_HEADS = 8
D_HEAD = 64
SEG = N_HEADS * D_HEAD
LANES = 128
SUBLANES = 8
HALO = 32
VMEM_LIMIT = 56 * 1024 * 1024


def _cparams(sem):
    return pltpu.CompilerParams(dimension_semantics=sem, vmem_limit_bytes=VMEM_LIMIT)


def _dot(a, b):
    return jnp.dot(a, b, preferred_element_type=F32)


def _dot_nt(a, b):
    return lax.dot_general(a, b, (((1,), (1,)), ((), ())), preferred_element_type=F32)


def _dot_tn(a, b):
    return lax.dot_general(a, b, (((0,), (0,)), ((), ())), preferred_element_type=F32)


def _split3(x):
    hi = x.astype(BF16)
    r1 = x - hi.astype(F32)
    mid = r1.astype(BF16)
    lo = (r1 - mid.astype(F32)).astype(BF16)
    return hi, mid, lo


def _dot_exact_rhs(a_f32, b_bf16):
    hi, mid, lo = _split3(a_f32)
    return _dot(hi, b_bf16) + _dot(mid, b_bf16) + _dot(lo, b_bf16)


def _dot_exact_lhs(a_bf16, b_f32):
    hi, mid, lo = _split3(b_f32)
    return _dot(a_bf16, hi) + _dot(a_bf16, mid) + _dot(a_bf16, lo)


def _sigmoid(x):
    return 1.0 / (1.0 + jnp.exp(-x))


def _silu(x):
    return x * _sigmoid(x)


def _rms_rows(x, g):
    return x * lax.rsqrt(jnp.mean(x * x, axis=-1, keepdims=True) + EPS) * g


def _inproj_kernel(x_ref, g_ref, w_ref, wff_ref, bff_ref, cos_ref, sin_ref, gq_ref, gk_ref, bd_ref,
                   u_ref, rq_ref, rk_ref, rv_ref, sg_ref, fq_ref, fk_ref, fkb_ref, fv_ref, fvb_ref, lf_ref):
    bb, tt, d = x_ref.shape
    tm = bb * tt
    x = x_ref[...].reshape(tm, d)
    h = _rms_rows(x, g_ref[...]).astype(BF16)

    def proj(i):
        return _dot(h, w_ref[:, i * SEG:(i + 1) * SEG])

    def put(ref, val):
        ref[...] = val.reshape(bb, tt, SEG).astype(ref.dtype)

    put(u_ref, proj(0) * _sigmoid(proj(1)))

    cos = cos_ref[...][None]
    sin = sin_ref[...][None]
    half = D_HEAD // 2
    lane = lax.broadcasted_iota(jnp.int32, (1, 1, LANES), 2)
    first_half = (lane % D_HEAD) < half

    def rotary(p, scale):
        cols = []
        for c in range(SEG // LANES):
            pc = p[:, c * LANES:(c + 1) * LANES]
            partner = jnp.where(first_half.reshape(1, LANES),
                                pltpu.roll(pc, LANES - half, 1), pltpu.roll(pc, half, 1))
            r = pc.reshape(bb, tt, LANES) * cos + partner.reshape(bb, tt, LANES) * sin
            cols.append(r * scale)
        return jnp.concatenate(cols, axis=-1)

    rq_ref[...] = rotary(proj(2), 1.0).astype(BF16)
    rk_ref[...] = rotary(proj(3), D_HEAD ** -0.5).astype(BF16)
    put(rv_ref, proj(4))
    put(sg_ref, _silu(proj(5)))

    bd = bd_ref[...]

    def head_rms(p, g):
        ms = _dot((p * p).astype(BF16), bd)
        return p * lax.rsqrt(ms + EPS) * g

    put(fq_ref, head_rms(proj(6), gq_ref[...]) * (D_HEAD ** -0.5))
    fk = head_rms(proj(7), gk_ref[...])
    put(fk_ref, fk)
    put(fkb_ref, fk)
    fv = proj(8)
    put(fv_ref, fv)
    put(fvb_ref, fv)

    z = _dot_nt(wff_ref[...], h) + bff_ref[...]
    lf_ref[0] = jnp.minimum(z, 0.0) - jnp.log(1.0 + jnp.exp(-jnp.abs(z)))


def _inproj(x, g, w_main, w_ffT, b_ff, cos_t, sin_t, gq, gk, bd, *, bb, tt):
    b, t, d = x.shape
    n_t, n_b = t // tt, b // bb
    tm = bb * tt
    act = lambda: pl.BlockSpec((bb, tt, SEG), lambda i, j: (j, i, 0))
    const = lambda shape: pl.BlockSpec(shape, lambda i, j: (0,) * len(shape))
    out_shape = ([jax.ShapeDtypeStruct((b, t, SEG), F32)]
                 + [jax.ShapeDtypeStruct((b, t, SEG), BF16)] * 5
                 + [jax.ShapeDtypeStruct((b, t, SEG), F32), jax.ShapeDtypeStruct((b, t, SEG), BF16)] * 2
                 + [jax.ShapeDtypeStruct((n_t * n_b, N_HEADS, tm), F32)])
    out_specs = [act() for _ in range(10)] + [pl.BlockSpec((1, N_HEADS, tm), lambda i, j: (i * n_b + j, 0, 0))]
    outs = pl.pallas_call(
        _inproj_kernel,
        grid=(n_t, n_b),
        in_specs=[pl.BlockSpec((bb, tt, d), lambda i, j: (j, i, 0)),
                  const((1, d)), const(w_main.shape), const(w_ffT.shape), const((N_HEADS, 1)),
                  pl.BlockSpec((tt, LANES), lambda i, j: (i, 0)), pl.BlockSpec((tt, LANES), lambda i, j: (i, 0)),
                  const((1, SEG)), const((1, SEG)), const((SEG, SEG))],
        out_specs=out_specs,
        out_shape=out_shape,
        compiler_params=_cparams(("arbitrary", "arbitrary")),
        name="inproj",
    )(x, g, w_main, w_ffT, b_ff, cos_t, sin_t, gq, gk, bd)
    lf = outs[-1].reshape(n_t, n_b, N_HEADS, bb, tt)
    lf = jnp.transpose(lf, (2, 1, 3, 0, 4)).reshape(N_HEADS, b, t)
    return outs[:-1], lf


def _conv_kernel(u_ref, st_ref, w_ref, b_ref, lg_ref, lb_ref, a_ref, xbuf, *, width, rows):
    tt = u_ref.shape[1]
    t = pl.program_id(1)

    @pl.when(t == 0)
    def _():
        xbuf[0:HALO, :] = st_ref[0]

    @pl.when(t > 0)
    def _():
        xbuf[0:HALO, :] = xbuf[tt:tt + HALO, :]

    xbuf[HALO:HALO + tt, :] = u_ref[0]
    first = HALO - (width - 1)
    bias = b_ref[...]
    lg = lg_ref[...]
    lb = lb_ref[...]
    for r0 in range(0, tt, rows):
        acc = jnp.zeros((rows, xbuf.shape[1]), F32) + bias
        for j in range(width):
            acc = acc + w_ref[j:j + 1, :] * xbuf[r0 + first + j:r0 + first + j + rows, :]
        mu = jnp.mean(acc, axis=-1, keepdims=True)
        xc = acc - mu
        var = jnp.mean(xc * xc, axis=-1, keepdims=True)
        y = xc * lax.rsqrt(var + EPS) * lg + lb
        a_ref[0, r0:r0 + rows, :] = _silu(y).astype(a_ref.dtype)


def _conv_branch(u, state, w_dw, b_dw, ln_g, ln_b, *, tt):
    b, t, c = u.shape
    width = w_dw.shape[0]
    st = jnp.pad(state.astype(F32), ((0, 0), (HALO - (width - 1), 0), (0, 0)))
    rows = min(tt, 32)
    const = lambda shape: pl.BlockSpec(shape, lambda i, j: (0,) * len(shape))
    return pl.pallas_call(
        functools.partial(_conv_kernel, width=width, rows=rows),
        grid=(b, t // tt),
        in_specs=[pl.BlockSpec((1, tt, c), lambda i, j: (i, j, 0)),
                  pl.BlockSpec((1, HALO, c), lambda i, j: (i, 0, 0)),
                  const((width, c)), const((1, c)), const((1, c)), const((1, c))],
        out_specs=pl.BlockSpec((1, tt, c), lambda i, j: (i, j, 0)),
        out_shape=jax.ShapeDtypeStruct((b, t, c), BF16),
        scratch_shapes=[pltpu.VMEM((HALO + tt, c), F32)],
        compiler_params=_cparams(("arbitrary", "arbitrary")),
        name="conv",
    )(u, st, w_dw, b_dw, ln_g, ln_b)


def _ret_kernel(q_ref, k_ref, v_ref, sg_ref, st0_ref, dmat_ref, qdec_ref, kdec_ref, gn_ref,
                o_ref, st_ref, oall, *, tile_decay):
    t = pl.program_id(1)

    @pl.when(t == 0)
    def _():
        st_ref[...] = st0_ref[...]

    q = q_ref[0]
    k = k_ref[0]
    q_in = (q.astype(F32) * qdec_ref[...]).astype(BF16)
    k_end = (k.astype(F32) * kdec_ref[...]).astype(BF16)
    for h in range(N_HEADS):
        sl = slice(h * D_HEAD, (h + 1) * D_HEAD)
        qh, kh, vh = q[:, sl], k[:, sl], v_ref[0, :, sl]
        s = _dot_nt(qh, kh) * dmat_ref[h]
        state = st_ref[0, h]
        o = _dot(s.astype(BF16), vh) + _dot(q_in[:, sl], state.astype(BF16))
        st_ref[0, h] = state * tile_decay[h] + _dot_tn(k_end[:, sl], vh)
        mu = jnp.mean(o, axis=-1, keepdims=True)
        oc = o - mu
        var = jnp.mean(oc * oc, axis=-1, keepdims=True)
        oall[:, sl] = oc * lax.rsqrt(var + EPS)
    o_ref[0] = (oall[...] * gn_ref[...] * sg_ref[0].astype(F32)).astype(o_ref.dtype)


def _ret_tables(rows, chunk):
    lg = np.log1p(-np.exp2(-5.0 - np.arange(N_HEADS, dtype=np.float64)))
    idx = np.arange(rows, dtype=np.float64)
    dist = idx[:, None] - idx[None, :]
    visible = (idx[None, :] // chunk) <= (idx[:, None] // chunk)
    dmat = np.where(visible[None], np.exp(lg[:, None, None] * np.abs(dist)[None]), 0.0)
    qdec = np.repeat(np.exp(lg[None, :] * (idx[:, None] + 1.0)), D_HEAD, axis=1)
    kdec = np.repeat(np.exp(lg[None, :] * (rows - 1.0 - idx[:, None])), D_HEAD, axis=1)
    tile_decay = tuple(float(v) for v in np.exp(lg * rows))
    return (jnp.asarray(dmat, F32), jnp.asarray(qdec, F32), jnp.asarray(kdec, F32), tile_decay)


def _retention(q, k, v, sg, state0, gn_g, *, rows):
    b, t, c = q.shape
    chunk = min(t, CHUNK)
    dmat, qdec, kdec, tile_decay = _ret_tables(rows, chunk)
    const = lambda shape: pl.BlockSpec(shape, lambda i, j: (0,) * len(shape))
    act = lambda: pl.BlockSpec((1, rows, c), lambda i, j: (i, j, 0))
    st_spec = lambda: pl.BlockSpec((1, N_HEADS, D_HEAD, D_HEAD), lambda i, j: (i, 0, 0, 0))
    return pl.pallas_call(
        functools.partial(_ret_kernel, tile_decay=tile_decay),
        grid=(b, t // rows),
        in_specs=[act(), act(), act(), act(), st_spec(),
                  const(dmat.shape), const(qdec.shape), const(kdec.shape), const((1, c))],
        out_specs=[act(), st_spec()],
        out_shape=[jax.ShapeDtypeStruct((b, t, c), BF16),
                   jax.ShapeDtypeStruct((b, N_HEADS, D_HEAD, D_HEAD), F32)],
        scratch_shapes=[pltpu.VMEM((rows, c), F32)],
        compiler_params=_cparams(("arbitrary", "arbitrary")),
        name="ret",
    )(q, k, v, sg, state0, dmat, qdec, kdec, gn_g)


def _cumsum_kernel(x_ref, o_ref):
    rows, length = x_ref.shape
    r = lax.broadcasted_iota(jnp.int32, (LANES, LANES), 0)
    c = lax.broadcasted_iota(jnp.int32, (LANES, LANES), 1)
    upper = (r <= c).astype(BF16)
    carry = jnp.zeros((rows, 1), F32)
    for i in range(length // LANES):
        seg = x_ref[:, i * LANES:(i + 1) * LANES]
        cs = _dot_exact_rhs(seg, upper) + carry
        o_ref[:, i * LANES:(i + 1) * LANES] = cs
        carry = cs[:, LANES - 1:LANES]


def _cumsum_lanes(x):
    return pl.pallas_call(
        _cumsum_kernel,
        out_shape=jax.ShapeDtypeStruct(x.shape, F32),
        compiler_params=pltpu.CompilerParams(vmem_limit_bytes=VMEM_LIMIT),
        name="cumsum",
    )(x)


def _fox_kernel(q_ref, k_ref, v_ref, cq_ref, ck_ref, o_ref, m_ref, l_ref, acc_ref):
    i = pl.program_id(2)
    j = pl.program_id(3)
    tq = q_ref.shape[1]
    tk = k_ref.shape[1]

    @pl.when(j == 0)
    def _():
        m_ref[...] = jnp.full(m_ref.shape, -jnp.inf, F32)
        l_ref[...] = jnp.zeros(l_ref.shape, F32)
        acc_ref[...] = jnp.zeros(acc_ref.shape, F32)

    def step(diagonal):
        q = q_ref[0]
        k = k_ref[0]
        v = v_ref[0]
        lane = lax.broadcasted_iota(jnp.int32, (1, LANES), 1)
        for hh in range(2):
            in_head = (lane // D_HEAD) == hh
            qm = jnp.where(in_head, q, jnp.zeros_like(q))
            s = _dot_nt(qm, k)
            bias = cq_ref[0, 0, hh:hh + 1, 0:1] - ck_ref[0, 0, hh:hh + 1, :]
            s = s + bias
            if diagonal:
                row = lax.broadcasted_iota(jnp.int32, (tq, tk), 0)
                col = lax.broadcasted_iota(jnp.int32, (tq, tk), 1)
                s = jnp.where(col <= row, s, -jnp.inf)
            m_old = m_ref[hh]
            m_new = jnp.maximum(m_old, jnp.max(s, axis=-1, keepdims=True))
            alpha = jnp.exp(m_old - m_new)
            p = jnp.exp(s - m_new)
            l_ref[hh] = alpha * l_ref[hh] + jnp.sum(p, axis=-1, keepdims=True)
            acc_ref[hh] = alpha * acc_ref[hh] + _dot(p.astype(BF16), v)
            m_ref[hh] = m_new

    @pl.when(j < i)
    def _():
        step(False)

    @pl.when(j == i)
    def _():
        step(True)
        lane = lax.broadcasted_iota(jnp.int32, (1, LANES), 1)
        o0 = acc_ref[0] / l_ref[0]
        o1 = acc_ref[1] / l_ref[1]
        o_ref[0] = jnp.where(lane < D_HEAD, o0, o1).astype(o_ref.dtype)


def _fox_prompt(q, k, v, cum, *, tq):
    b, t, c = q.shape
    n_pairs = c // LANES
    n_q = t // tq
    kv_map = lambda bi, hp, i, j: (bi, jnp.minimum(j, i), hp)
    return pl.pallas_call(
        _fox_kernel,
        grid=(b, n_pairs, n_q, n_q),
        in_specs=[pl.BlockSpec((1, tq, LANES), lambda bi, hp, i, j: (bi, i, hp)),
                  pl.BlockSpec((1, tq, LANES), kv_map),
                  pl.BlockSpec((1, tq, LANES), kv_map),
                  pl.BlockSpec((1, 1, 2, tq), lambda bi, hp, i, j: (bi, hp, 0, i)),
                  pl.BlockSpec((1, 1, 2, tq), lambda bi, hp, i, j: (bi, hp, 0, jnp.minimum(j, i)))],
        out_specs=pl.BlockSpec((1, tq, LANES), lambda bi, hp, i, j: (bi, i, hp)),
        out_shape=jax.ShapeDtypeStruct((b, t, c), BF16),
        scratch_shapes=[pltpu.VMEM((2, tq, 1), F32), pltpu.VMEM((2, tq, 1), F32),
                        pltpu.VMEM((2, tq, LANES), F32)],
        compiler_params=_cparams(("arbitrary", "arbitrary", "arbitrary", "arbitrary")),
        name="fox",
    )(q, k, v, cum, cum)


def _fox_cache_kernel(q_ref, kn_ref, vn_ref, kp_ref, vp_ref, lfp_ref, lfn_ref, o_ref, cum_ref, *, blk):
    t = q_ref.shape[1]
    p_len = kp_ref.shape[1]
    c = q_ref.shape[2]
    n_cols = N_HEADS * t

    r = lax.broadcasted_iota(jnp.int32, (blk, blk), 0)
    cc = lax.broadcasted_iota(jnp.int32, (blk, blk), 1)
    lower = (cc <= r).astype(BF16)
    carry = jnp.zeros((1, N_HEADS), F32)
    for i in range(p_len // blk):
        cs = _dot_exact_lhs(lower, lfp_ref[0, i * blk:(i + 1) * blk, :]) + carry
        cum_ref[i * blk:(i + 1) * blk, :] = cs
        carry = cs[blk - 1:blk, :]
    cum_new = _dot_exact_lhs(lower[:t, :t], lfn_ref[0]) + carry

    eh = lax.broadcasted_iota(jnp.int32, (N_HEADS, n_cols), 0)
    ec = lax.broadcasted_iota(jnp.int32, (N_HEADS, n_cols), 1)
    expand = (ec // t == eh).astype(BF16)

    q = q_ref[0]
    qrep = jnp.concatenate([q] * N_HEADS, axis=0)
    qr = lax.broadcasted_iota(jnp.int32, (n_cols, c), 0)
    ql = lax.broadcasted_iota(jnp.int32, (n_cols, c), 1)
    qbd = jnp.where(qr // t == ql // D_HEAD, qrep, jnp.zeros_like(qrep))

    s_past = _dot_nt(kp_ref[0].astype(BF16), qbd) - _dot_exact_rhs(cum_ref[...], expand)
    s_new = _dot_nt(kn_ref[0], qbd) - _dot_exact_rhs(cum_new, expand)
    key = lax.broadcasted_iota(jnp.int32, (t, n_cols), 0)
    qry = lax.broadcasted_iota(jnp.int32, (t, n_cols), 1) % t
    s_new = jnp.where(key <= qry, s_new, -jnp.inf)

    m = jnp.maximum(jnp.max(s_past, axis=0, keepdims=True), jnp.max(s_new, axis=0, keepdims=True))
    p_past = jnp.exp(s_past - m)
    p_new = jnp.exp(s_new - m)
    inv = 1.0 / (jnp.sum(p_past, axis=0, keepdims=True) + jnp.sum(p_new, axis=0, keepdims=True))
    p_past = (p_past * inv).astype(BF16)
    p_new = (p_new * inv).astype(BF16)
    o_full = _dot_tn(p_past, vp_ref[0].astype(BF16)) + _dot_tn(p_new, vn_ref[0])
    for h in range(N_HEADS):
        o_ref[0, :, h * D_HEAD:(h + 1) * D_HEAD] = (
            o_full[h * t:(h + 1) * t, h * D_HEAD:(h + 1) * D_HEAD].astype(o_ref.dtype))


def _fox_cache(q, kn, vn, kp, vp, lf_past, lf_new):
    b, t, c = q.shape
    p_len = kp.shape[1]
    blk = 256
    assert p_len % blk == 0 and t <= blk
    new = lambda: pl.BlockSpec((1, t, c), lambda i: (i, 0, 0))
    past = lambda: pl.BlockSpec((1, p_len, c), lambda i: (i, 0, 0))
    return pl.pallas_call(
        functools.partial(_fox_cache_kernel, blk=blk),
        grid=(b,),
        in_specs=[new(), new(), new(), past(), past(),
                  pl.BlockSpec((1, p_len, N_HEADS), lambda i: (i, 0, 0)),
                  pl.BlockSpec((1, t, N_HEADS), lambda i: (i, 0, 0))],
        out_specs=new(),
        out_shape=jax.ShapeDtypeStruct((b, t, c), BF16),
        scratch_shapes=[pltpu.VMEM((p_len, N_HEADS), F32)],
        compiler_params=_cparams(("arbitrary",)),
        name="fox_cache",
    )(q, kn, vn, kp, vp, lf_past, lf_new)


def _merge_kernel(x_ref, g_ref, wg_ref, a_ref, b_ref, c_ref, wa_ref, wb_ref, wc_ref, wo_ref, o_ref):
    x = x_ref[...]
    d = x.shape[1]
    h = _rms_rows(x, g_ref[...]).astype(BF16)
    merged = jnp.zeros(x.shape, F32)
    for n, (br_ref, w_ref) in enumerate(((a_ref, wa_ref), (b_ref, wb_ref), (c_ref, wc_ref))):
        gate = _sigmoid(_dot(h, wg_ref[:, n * d:(n + 1) * d]))
        merged = merged + gate * _dot(br_ref[...], w_ref[...])
    o_ref[...] = x + _dot(merged.astype(BF16), wo_ref[...])


def _merge(x2d, g, w_gates, a, bm, c, wa, wb, wc, wo, *, tm):
    n, d = x2d.shape
    seg = a.shape[1]
    const = lambda shape: pl.BlockSpec(shape, lambda i: (0,) * len(shape))
    row = lambda width: pl.BlockSpec((tm, width), lambda i: (i, 0))
    return pl.pallas_call(
        _merge_kernel,
        grid=(n // tm,),
        in_specs=[row(d), const((1, d)), const(w_gates.shape), row(seg), row(seg), row(seg),
                  const(wa.shape), const(wb.shape), const(wc.shape), const(wo.shape)],
        out_specs=row(d),
        out_shape=jax.ShapeDtypeStruct((n, d), F32),
        compiler_params=_cparams(("arbitrary",)),
        name="merge",
    )(x2d, g, w_gates, a, bm, c, wa, wb, wc, wo)


def _erf(x):
    return lax.erf(x)


def _ffn_kernel(x_ref, g_ref, wup_ref, st_ref, wdw_ref, bdw_ref, wdn_ref, o_ref, buf_ref, abuf, *, n_chunks):
    tt, d = x_ref.shape[1], x_ref.shape[2]
    dff = wdn_ref.shape[0]
    cf = dff // n_chunks
    width = wdw_ref.shape[0]
    t = pl.program_id(1)
    x = x_ref[0]
    h = _rms_rows(x, g_ref[...]).astype(BF16)

    @pl.when(t == 0)
    def _():
        abuf[0:SUBLANES, :] = st_ref[0]

    @pl.when(t > 0)
    def _():
        abuf[0:SUBLANES, :] = abuf[tt:tt + SUBLANES, :]

    out = x
    first = SUBLANES - (width - 1)
    for n in range(n_chunks):
        cs = slice(n * cf, (n + 1) * cf)
        abuf[SUBLANES:SUBLANES + tt, cs] = _dot(h, wup_ref[:, cs])
        gate = _dot(h, wup_ref[:, dff + n * cf:dff + (n + 1) * cf])
        conv = jnp.zeros((tt, cf), F32) + bdw_ref[:, cs]
        for j in range(width):
            conv = conv + wdw_ref[j:j + 1, cs] * abuf[first + j:first + j + tt, cs]
        act = 0.5 * conv * (1.0 + _erf(conv * (2.0 ** -0.5))) * gate
        out = out + _dot(act.astype(BF16), wdn_ref[cs, :])
    o_ref[0] = out
    buf_ref[0] = abuf[SUBLANES + tt - (width - 1):SUBLANES + tt, :]


def _ffn(x, g, w_up, state, w_dw, b_dw, w_down, *, tt, n_chunks):
    b, t, d = x.shape
    dff = w_down.shape[0]
    width = w_dw.shape[0]
    st = jnp.pad(state.astype(F32), ((0, 0), (SUBLANES - (width - 1), 0), (0, 0)))
    const = lambda shape: pl.BlockSpec(shape, lambda i, j: (0,) * len(shape), pipeline_mode=pl.Buffered(1))
    return pl.pallas_call(
        functools.partial(_ffn_kernel, n_chunks=n_chunks),
        grid=(b, t // tt),
        in_specs=[pl.BlockSpec((1, tt, d), lambda i, j: (i, j, 0)),
                  const((1, d)), const(w_up.shape),
                  pl.BlockSpec((1, SUBLANES, dff), lambda i, j: (i, 0, 0)),
                  const(w_dw.shape), const((1, dff)), const(w_down.shape)],
        out_specs=[pl.BlockSpec((1, tt, d), lambda i, j: (i, j, 0)),
                   pl.BlockSpec((1, width - 1, dff), lambda i, j: (i, 0, 0))],
        out_shape=[jax.ShapeDtypeStruct((b, t, d), F32),
                   jax.ShapeDtypeStruct((b, width - 1, dff), F32)],
        scratch_shapes=[pltpu.VMEM((SUBLANES + tt, dff), F32)],
        compiler_params=_cparams(("arbitrary", "arbitrary")),
        name="ffn",
    )(x, g, w_up, st, w_dw, b_dw, w_down)


def _rope_tables(pos):
    half = D_HEAD // 2
    inv = jnp.exp(-math.log(ROPE_BASE) * jnp.arange(half, dtype=F32) / half)
    ang = pos.astype(F32)[:, None] * inv[None, :]
    cos, sin = jnp.cos(ang), jnp.sin(ang)
    cos_h = jnp.concatenate([cos, cos], axis=1)
    sin_h = jnp.concatenate([-sin, sin], axis=1)
    reps = LANES // D_HEAD
    return jnp.tile(cos_h, (1, reps)), jnp.tile(sin_h, (1, reps))


def _layer(x, pos, conv_state, ret_state, fox_past, ffn_state, lw, cfg):
    (g_mix, w_in, b_fgate, w_dw_conv, b_dw_conv, ln_conv_g, ln_conv_b, w_conv_out,
     gn_ret_g, w_ret_out, g_q_fox, g_k_fox, w_fox_out, w_out,
     g_ffn, w_ffn_up, w_ffn_dw, b_ffn_dw, w_ffn_down) = lw
    b, t, d = x.shape
    n_main = 9 * SEG
    row = lambda v: v.reshape(1, -1).astype(F32)

    w_main = w_in[:, :n_main].astype(BF16)
    w_ffT = jnp.transpose(w_in[:, n_main:n_main + N_HEADS]).astype(BF16)
    w_gates = w_in[:, n_main + N_HEADS:].astype(BF16)
    cos_t, sin_t = _rope_tables(pos)
    head_id = np.arange(SEG) // D_HEAD
    bd = jnp.asarray((head_id[:, None] == head_id[None, :]) / D_HEAD, BF16)
    tile_head = lambda v: jnp.tile(v.astype(F32), N_HEADS).reshape(1, SEG)

    (u, rq, rk, rv, sg, fq, fk, fkb, fv, fvb), lf = _inproj(
        x, row(g_mix), w_main, w_ffT, b_fgate.reshape(N_HEADS, 1).astype(F32), cos_t, sin_t,
        tile_head(g_q_fox), tile_head(g_k_fox), bd, bb=cfg["bb"], tt=cfg["tt"])

    a_br = _conv_branch(u, conv_state, w_dw_conv.astype(F32), row(b_dw_conv), row(ln_conv_g), row(ln_conv_b),
                        tt=cfg["tt_seq"])
    new_conv = u[:, t - (w_dw_conv.shape[0] - 1):, :]

    b_br, new_ret = _retention(rq, rk, rv, sg, ret_state.astype(F32), row(gn_ret_g), rows=cfg["ret_rows"])

    logf = jnp.transpose(lf, (1, 2, 0))
    if fox_past is None:
        cum = _cumsum_lanes(lf.reshape(N_HEADS * b, t)).reshape(N_HEADS // 2, 2, b, t)
        cum = jnp.transpose(cum, (2, 0, 1, 3))
        c_br = _fox_prompt(fq, fkb, fvb, cum, tq=cfg["tq"])
    else:
        k_past, v_past, logf_past = fox_past
        p_len = k_past.shape[1]
        c_br = _fox_cache(fq, fkb, fvb, k_past.reshape(b, p_len, SEG), v_past.reshape(b, p_len, SEG),
                          logf_past.astype(F32), logf)

    n = b * t
    x1 = _merge(x.reshape(n, d), row(g_mix), w_gates, a_br.reshape(n, SEG), b_br.reshape(n, SEG),
                c_br.reshape(n, SEG), w_conv_out.astype(BF16), w_ret_out.astype(BF16),
                w_fox_out.astype(BF16), w_out.astype(BF16), tm=cfg["tm"]).reshape(b, t, d)

    x2, new_ffn = _ffn(x1, row(g_ffn), w_ffn_up.astype(BF16), ffn_state, w_ffn_dw.astype(F32), row(b_ffn_dw),
                       w_ffn_down.astype(BF16), tt=cfg["tt_seq"], n_chunks=cfg["ffn_chunks"])

    states = (new_conv, new_ret, fk.reshape(b, t, N_HEADS, D_HEAD), fv.reshape(b, t, N_HEADS, D_HEAD),
              logf, new_ffn)
    return x2, states


def _group_cfg(b, t):
    if t >= 512:
        return dict(bb=1, tt=512, tt_seq=512, ret_rows=256, tq=512, tm=512, ffn_chunks=2)
    bb = max(1, min(b, 512 // t))
    return dict(bb=bb, tt=t, tt_seq=t, ret_rows=t, tq=t, tm=min(b * t, 512), ffn_chunks=2)


def kernel(x_prompt, x_sample, state_conv, state_ret, cache_fox_k, cache_fox_v, cache_fox_logf, state_ffn_conv, g_mix, w_in, b_fgate, w_dw_conv, b_dw_conv, ln_conv_g, ln_conv_b, w_conv_out, gn_ret_g, w_ret_out, g_q_fox, g_k_fox, w_fox_out, w_out, g_ffn, w_ffn_up, w_ffn_dw, b_ffn_dw, w_ffn_down):
    bp, sp, _ = x_prompt.shape
    bs, ts, _ = x_sample.shape
    depth = w_in.shape[0]
    p_len = cache_fox_k.shape[2]
    conv_w = w_dw_conv.shape[1]
    ffn_w = w_ffn_dw.shape[1]
    d_conv = w_dw_conv.shape[2]
    d_ff = w_ffn_down.shape[1]
    assert sp >= conv_w - 1 and ts >= conv_w - 1 and conv_w - 1 <= HALO and ffn_w - 1 <= SUBLANES
    assert d_conv == SEG and w_in.shape[2] == 9 * SEG + N_HEADS + 3 * x_prompt.shape[2]
    pos_p = jnp.arange(sp)
    pos_s = p_len + jnp.arange(ts)
    cfg_p = _group_cfg(bp, sp)
    cfg_s = _group_cfg(bs, ts)
    weights = (g_mix, w_in, b_fgate, w_dw_conv, b_dw_conv, ln_conv_g, ln_conv_b, w_conv_out,
               gn_ret_g, w_ret_out, g_q_fox, g_k_fox, w_fox_out, w_out,
               g_ffn, w_ffn_up, w_ffn_dw, b_ffn_dw, w_ffn_down)
    xp, xs = x_prompt, x_sample
    st_p, st_s = [], []
    for l in range(depth):
        lw = tuple(w[l] for w in weights)
        xp, sp_l = _layer(xp, pos_p,
                          jnp.zeros((bp, conv_w - 1, d_conv), F32),
                          jnp.zeros((bp, N_HEADS, D_HEAD, D_HEAD), F32),
                          None,
                          jnp.zeros((bp, ffn_w - 1, d_ff), F32), lw, cfg_p)
        xs, ss_l = _layer(xs, pos_s, state_conv[l], state_ret[l],
                          (cache_fox_k[l], cache_fox_v[l], cache_fox_logf[l]),
                          state_ffn_conv[l], lw, cfg_s)
        st_p.append(sp_l)
        st_s.append(ss_l)
    stack = lambda sts: tuple(jnp.stack([s[i] for s in sts]) for i in range(6))
    return (xp, xs) + stack(st_p) + stack(st_s)
```

```python
import functools
import math

import jax
import jax.numpy as jnp
import numpy as np
from jax import lax
from jax.experimental import pallas as pl
from jax.experimental.pallas import tpu as pltpu

F32 = jnp.float32
BF16 = jnp.bfloat16

EPS = 1e-6
ROPE_BASE = 10000.0
CHUNK = 64
N_HEADS = 8
D_HEAD = 64
SEG = N_HEADS * D_HEAD
LANES = 128
SUBLANES = 8
HALO = 32
VMEM_LIMIT = 56 * 1024 * 1024


def _cparams(sem):
    return pltpu.CompilerParams(dimension_semantics=sem, vmem_limit_bytes=VMEM_LIMIT)


def _dot(a, b):
    return jnp.dot(a, b, preferred_element_type=F32)


def _dot_nt(a, b):
    return lax.dot_general(a, b, (((1,), (1,)), ((), ())), preferred_element_type=F32)


def _dot_tn(a, b):
    return lax.dot_general(a, b, (((0,), (0,)), ((), ())), preferred_element_type=F32)


def _split3(x):
    hi = x.astype(BF16)
    r1 = x - hi.astype(F32)
    mid = r1.astype(BF16)
    lo = (r1 - mid.astype(F32)).astype(BF16)
    return hi, mid, lo


def _dot_exact_rhs(a_f32, b_bf16):
    hi, mid, lo = _split3(a_f32)
    return _dot(hi, b_bf16) + _dot(mid, b_bf16) + _dot(lo, b_bf16)


def _dot_exact_lhs(a_bf16, b_f32):
    hi, mid, lo = _split3(b_f32)
    return _dot(a_bf16, hi) + _dot(a_bf16, mid) + _dot(a_bf16, lo)


def _sigmoid(x):
    return 1.0 / (1.0 + jnp.exp(-x))


def _silu(x):
    return x * _sigmoid(x)


def _rms_rows(x, g):
    return x * lax.rsqrt(jnp.mean(x * x, axis=-1, keepdims=True) + EPS) * g


def _inproj_kernel(x_ref, g_ref, w_ref, wff_ref, bff_ref, cos_ref, sin_ref, gq_ref, gk_ref, bd_ref, *rest,
                   transposed):
    if transposed:
        wqT_ref, wvT_ref, gqT_ref = rest[:3]
        rest = rest[3:]
    u_ref, rq_ref, rk_ref, rv_ref, sg_ref, fq_ref, fk_ref, fkb_ref, fv_ref, fvb_ref, lf_ref = rest
    bb, tt, d = x_ref.shape
    tm = bb * tt
    x = x_ref[...].reshape(tm, d)
    h = _rms_rows(x, g_ref[...]).astype(BF16)

    def proj(i):
        return _dot(h, w_ref[:, i * SEG:(i + 1) * SEG])

    def put(ref, val):
        ref[...] = val.reshape(bb, tt, SEG).astype(ref.dtype)

    put(u_ref, proj(0) * _sigmoid(proj(1)))

    cos = cos_ref[...][None]
    sin = sin_ref[...][None]
    half = D_HEAD // 2
    lane = lax.broadcasted_iota(jnp.int32, (1, 1, LANES), 2)
    first_half = (lane % D_HEAD) < half

    def rotary(p, scale):
        cols = []
        for c in range(SEG // LANES):
            pc = p[:, c * LANES:(c + 1) * LANES]
            partner = jnp.where(first_half.reshape(1, LANES),
                                pltpu.roll(pc, LANES - half, 1), pltpu.roll(pc, half, 1))
            r = pc.reshape(bb, tt, LANES) * cos + partner.reshape(bb, tt, LANES) * sin
            cols.append(r * scale)
        return jnp.concatenate(cols, axis=-1)

    rq_ref[...] = rotary(proj(2), 1.0).astype(BF16)
    rk_ref[...] = rotary(proj(3), D_HEAD ** -0.5).astype(BF16)
    put(rv_ref, proj(4))
    put(sg_ref, _silu(proj(5)))

    bd = bd_ref[...]

    def head_rms(p, g):
        ms = _dot((p * p).astype(BF16), bd)
        return p * lax.rsqrt(ms + EPS) * g

    fk = head_rms(proj(7), gk_ref[...])
    put(fk_ref, fk)
    put(fkb_ref, fk)
    fv = proj(8)
    put(fv_ref, fv)
    if transposed:
        zq = _dot_nt(wqT_ref[...], h).reshape(N_HEADS, D_HEAD, tm)
        ms = jnp.mean(zq * zq, axis=1, keepdims=True)
        gq = jnp.concatenate([gqT_ref[...]] * (tm // LANES), axis=1).reshape(N_HEADS, D_HEAD, tm)
        fq_ref[0] = (zq * lax.rsqrt(ms + EPS) * gq * (D_HEAD ** -0.5)).reshape(SEG, tm).astype(BF16)
        fvb_ref[0] = _dot_nt(wvT_ref[...], h).astype(BF16)
    else:
        put(fq_ref, head_rms(proj(6), gq_ref[...]) * (D_HEAD ** -0.5))
        put(fvb_ref, fv)

    z = _dot_nt(wff_ref[...], h) + bff_ref[...]
    lf_ref[0] = jnp.minimum(z, 0.0) - jnp.log(1.0 + jnp.exp(-jnp.abs(z)))


def _inproj(x, g, w_main, w_ffT, b_ff, cos_t, sin_t, gq, gk, bd, transposed_ops, *, bb, tt):
    b, t, d = x.shape
    n_t, n_b = t // tt, b // bb
    tm = bb * tt
    transposed = transposed_ops is not None
    assert not transposed or bb == 1
    act = lambda: pl.BlockSpec((bb, tt, SEG), lambda i, j: (j, i, 0))
    act_t = lambda: pl.BlockSpec((1, SEG, tt), lambda i, j: (j, 0, i))
    const = lambda shape: pl.BlockSpec(shape, lambda i, j: (0,) * len(shape))
    nat = lambda dt: jax.ShapeDtypeStruct((b, t, SEG), dt)
    tr = jax.ShapeDtypeStruct((b, SEG, t), BF16)
    out_shape = [nat(F32), nat(BF16), nat(BF16), nat(BF16), nat(BF16),
                 tr if transposed else nat(BF16),
                 nat(F32), nat(BF16), nat(F32),
                 tr if transposed else nat(BF16),
                 jax.ShapeDtypeStruct((n_t * n_b, N_HEADS, tm), F32)]
    out_specs = [act() for _ in range(10)] + [pl.BlockSpec((1, N_HEADS, tm), lambda i, j: (i * n_b + j, 0, 0))]
    if transposed:
        out_specs[5] = act_t()
        out_specs[9] = act_t()
    extra = list(transposed_ops) if transposed else []
    outs = pl.pallas_call(
        functools.partial(_inproj_kernel, transposed=transposed),
        grid=(n_t, n_b),
        in_specs=[pl.BlockSpec((bb, tt, d), lambda i, j: (j, i, 0)),
                  const((1, d)), const(w_main.shape), const(w_ffT.shape), const((N_HEADS, 1)),
                  pl.BlockSpec((tt, LANES), lambda i, j: (i, 0)), pl.BlockSpec((tt, LANES), lambda i, j: (i, 0)),
                  const((1, SEG)), const((1, SEG)), const((SEG, SEG))] + [const(e.shape) for e in extra],
        out_specs=out_specs,
        out_shape=out_shape,
        compiler_params=_cparams(("arbitrary", "arbitrary")),
        name="inproj",
    )(x, g, w_main, w_ffT, b_ff, cos_t, sin_t, gq, gk, bd, *extra)
    lf = outs[-1].reshape(n_t, n_b, N_HEADS, bb, tt)
    lf = jnp.transpose(lf, (2, 1, 3, 0, 4)).reshape(N_HEADS, b, t)
    return outs[:-1], lf


def _conv_kernel(u_ref, st_ref, w_ref, b_ref, lg_ref, lb_ref, a_ref, xbuf, *, width, rows):
    tt = u_ref.shape[1]
    t = pl.program_id(1)

    @pl.when(t == 0)
    def _():
        xbuf[0:HALO, :] = st_ref[0]

    @pl.when(t > 0)
    def _():
        xbuf[0:HALO, :] = xbuf[tt:tt + HALO, :]

    xbuf[HALO:HALO + tt, :] = u_ref[0]
    first = HALO - (width - 1)
    bias = b_ref[...]
    lg = lg_ref[...]
    lb = lb_ref[...]
    for r0 in range(0, tt, rows):
        acc = jnp.zeros((rows, xbuf.shape[1]), F32) + bias
        for j in range(width):
            acc = acc + w_ref[j:j + 1, :] * xbuf[r0 + first + j:r0 + first + j + rows, :]
        mu = jnp.mean(acc, axis=-1, keepdims=True)
        xc = acc - mu
        var = jnp.mean(xc * xc, axis=-1, keepdims=True)
        y = xc * lax.rsqrt(var + EPS) * lg + lb
        a_ref[0, r0:r0 + rows, :] = _silu(y).astype(a_ref.dtype)


def _conv_branch(u, state, w_dw, b_dw, ln_g, ln_b, *, tt):
    b, t, c = u.shape
    width = w_dw.shape[0]
    st = jnp.pad(state.astype(F32), ((0, 0), (HALO - (width - 1), 0), (0, 0)))
    rows = min(tt, 32)
    const = lambda shape: pl.BlockSpec(shape, lambda i, j: (0,) * len(shape))
    return pl.pallas_call(
        functools.partial(_conv_kernel, width=width, rows=rows),
        grid=(b, t // tt),
        in_specs=[pl.BlockSpec((1, tt, c), lambda i, j: (i, j, 0)),
                  pl.BlockSpec((1, HALO, c), lambda i, j: (i, 0, 0)),
                  const((width, c)), const((1, c)), const((1, c)), const((1, c))],
        out_specs=pl.BlockSpec((1, tt, c), lambda i, j: (i, j, 0)),
        out_shape=jax.ShapeDtypeStruct((b, t, c), BF16),
        scratch_shapes=[pltpu.VMEM((HALO + tt, c), F32)],
        compiler_params=_cparams(("arbitrary", "arbitrary")),
        name="conv",
    )(u, st, w_dw, b_dw, ln_g, ln_b)


def _ret_kernel(q_ref, k_ref, v_ref, sg_ref, st0_ref, dmat_ref, qdec_ref, kdec_ref, gn_ref,
                o_ref, st_ref, oall, *, tile_decay):
    t = pl.program_id(1)

    @pl.when(t == 0)
    def _():
        st_ref[...] = st0_ref[...]

    q = q_ref[0]
    k = k_ref[0]
    q_in = (q.astype(F32) * qdec_ref[...]).astype(BF16)
    k_end = (k.astype(F32) * kdec_ref[...]).astype(BF16)
    for h in range(N_HEADS):
        sl = slice(h * D_HEAD, (h + 1) * D_HEAD)
        qh, kh, vh = q[:, sl], k[:, sl], v_ref[0, :, sl]
        s = _dot_nt(qh, kh) * dmat_ref[h]
        state = st_ref[0, h]
        o = _dot(s.astype(BF16), vh) + _dot(q_in[:, sl], state.astype(BF16))
        st_ref[0, h] = state * tile_decay[h] + _dot_tn(k_end[:, sl], vh)
        mu = jnp.mean(o, axis=-1, keepdims=True)
        oc = o - mu
        var = jnp.mean(oc * oc, axis=-1, keepdims=True)
        oall[:, sl] = oc * lax.rsqrt(var + EPS)
    o_ref[0] = (oall[...] * gn_ref[...] * sg_ref[0].astype(F32)).astype(o_ref.dtype)


def _ret_tables(rows, chunk):
    lg = np.log1p(-np.exp2(-5.0 - np.arange(N_HEADS, dtype=np.float64)))
    idx = np.arange(rows, dtype=np.float64)
    dist = idx[:, None] - idx[None, :]
    visible = (idx[None, :] // chunk) <= (idx[:, None] // chunk)
    dmat = np.where(visible[None], np.exp(lg[:, None, None] * np.abs(dist)[None]), 0.0)
    qdec = np.repeat(np.exp(lg[None, :] * (idx[:, None] + 1.0)), D_HEAD, axis=1)
    kdec = np.repeat(np.exp(lg[None, :] * (rows - 1.0 - idx[:, None])), D_HEAD, axis=1)
    tile_decay = tuple(float(v) for v in np.exp(lg * rows))
    return (jnp.asarray(dmat, F32), jnp.asarray(qdec, F32), jnp.asarray(kdec, F32), tile_decay)


def _retention(q, k, v, sg, state0, gn_g, *, rows):
    b, t, c = q.shape
    chunk = min(t, CHUNK)
    dmat, qdec, kdec, tile_decay = _ret_tables(rows, chunk)
    const = lambda shape: pl.BlockSpec(shape, lambda i, j: (0,) * len(shape))
    act = lambda: pl.BlockSpec((1, rows, c), lambda i, j: (i, j, 0))
    st_spec = lambda: pl.BlockSpec((1, N_HEADS, D_HEAD, D_HEAD), lambda i, j: (i, 0, 0, 0))
    return pl.pallas_call(
        functools.partial(_ret_kernel, tile_decay=tile_decay),
        grid=(b, t // rows),
        in_specs=[act(), act(), act(), act(), st_spec(),
                  const(dmat.shape), const(qdec.shape), const(kdec.shape), const((1, c))],
        out_specs=[act(), st_spec()],
        out_shape=[jax.ShapeDtypeStruct((b, t, c), BF16),
                   jax.ShapeDtypeStruct((b, N_HEADS, D_HEAD, D_HEAD), F32)],
        scratch_shapes=[pltpu.VMEM((rows, c), F32)],
        compiler_params=_cparams(("arbitrary", "arbitrary")),
        name="ret",
    )(q, k, v, sg, state0, dmat, qdec, kdec, gn_g)


CUM_BLOCK = 256
AUG_PIECES = 3


def _cumaug_kernel(lf_ref, o_ref):
    t = lf_ref.shape[1]
    blk = min(CUM_BLOCK, t)
    r = lax.broadcasted_iota(jnp.int32, (blk, blk), 0)
    c = lax.broadcasted_iota(jnp.int32, (blk, blk), 1)
    lower = (c <= r).astype(BF16)
    sh = lax.broadcasted_iota(jnp.int32, (N_HEADS, LANES), 0)
    sl = lax.broadcasted_iota(jnp.int32, (N_HEADS, LANES), 1)
    place = [(sl == AUG_PIECES * sh + p).astype(BF16) for p in range(AUG_PIECES)]
    carry = jnp.zeros((1, N_HEADS), F32)
    for i in range(t // blk):
        cs = _dot_exact_lhs(lower, lf_ref[0, i * blk:(i + 1) * blk, :]) + carry
        carry = cs[blk - 1:blk, :]
        pieces = _split3(-cs)
        aug = sum(_dot(piece, sel) for piece, sel in zip(pieces, place))
        o_ref[0, i * blk:(i + 1) * blk, :] = aug.astype(o_ref.dtype)


def _cumaug(logf):
    b, t, h = logf.shape
    return pl.pallas_call(
        _cumaug_kernel,
        grid=(b,),
        in_specs=[pl.BlockSpec((1, t, h), lambda i: (i, 0, 0))],
        out_specs=pl.BlockSpec((1, t, LANES), lambda i: (i, 0, 0)),
        out_shape=jax.ShapeDtypeStruct((b, t, LANES), BF16),
        compiler_params=_cparams(("arbitrary",)),
        name="cumaug",
    )(logf)


ONES_ROWS = 16


def _fox_kernel(qi_ref, kj_ref, qT_ref, k_ref, ca_ref, vT_ref, o_ref, qaug, m_ref, acc_ref):
    hp = pl.program_id(1)
    s = pl.program_id(2)
    i = qi_ref[s]
    j = kj_ref[s]
    tq = qT_ref.shape[2]
    tk = k_ref.shape[1]

    @pl.when(j == 0)
    def _():
        qT = qT_ref[0]
        row = lax.broadcasted_iota(jnp.int32, (LANES, tq), 0)
        for hh in range(2):
            qaug[hh, 0:LANES, :] = jnp.where(row // D_HEAD == hh, qT, jnp.zeros_like(qT))
            first = AUG_PIECES * (2 * hp + hh)
            pick = (row >= first) & (row < first + AUG_PIECES)
            qaug[hh, LANES:2 * LANES, :] = jnp.where(pick, 1.0, 0.0).astype(BF16)
        m_ref[...] = jnp.full(m_ref.shape, -jnp.inf, F32)
        acc_ref[...] = jnp.zeros(acc_ref.shape, F32)

    def step(diagonal):
        kk = jnp.concatenate([k_ref[0], ca_ref[0]], axis=1)
        vT = vT_ref[0]
        ones = jnp.ones((ONES_ROWS, tk), BF16)
        for hh in range(2):
            sT = _dot(kk, qaug[hh])
            if diagonal:
                key = lax.broadcasted_iota(jnp.int32, (tk, tq), 0)
                qry = lax.broadcasted_iota(jnp.int32, (tk, tq), 1)
                sT = jnp.where(key <= qry, sT, -jnp.inf)
            m_old = m_ref[hh]
            m_new = jnp.maximum(m_old, jnp.max(sT, axis=0, keepdims=True))
            alpha = jnp.exp(m_old - m_new)
            p = jnp.exp(sT - m_new).astype(BF16)
            lhs = jnp.concatenate([vT[hh * D_HEAD:(hh + 1) * D_HEAD, :], ones], axis=0)
            acc_ref[hh] = alpha * acc_ref[hh] + _dot(lhs, p)
            m_ref[hh] = m_new

    @pl.when(j < i)
    def _():
        step(False)

    @pl.when(j == i)
    def _():
        step(True)
        outs = []
        for hh in range(2):
            acc = acc_ref[hh]
            outs.append(acc[0:D_HEAD, :] / acc[D_HEAD:D_HEAD + 1, :])
        o_ref[0] = jnp.transpose(jnp.concatenate(outs, axis=0)).astype(o_ref.dtype)


def _fox_prompt(qT, k, vT, caug, *, tq):
    b, c, t = qT.shape
    n_pairs = c // LANES
    n_q = t // tq
    pairs = [(i, j) for i in range(n_q) for j in range(i + 1)]
    qi = jnp.asarray([p[0] for p in pairs], jnp.int32)
    kj = jnp.asarray([p[1] for p in pairs], jnp.int32)
    grid_spec = pltpu.PrefetchScalarGridSpec(
        num_scalar_prefetch=2,
        grid=(b, n_pairs, len(pairs)),
        in_specs=[pl.BlockSpec((1, LANES, tq), lambda bi, hp, s, qi, kj: (bi, hp, qi[s])),
                  pl.BlockSpec((1, tq, LANES), lambda bi, hp, s, qi, kj: (bi, kj[s], hp)),
                  pl.BlockSpec((1, tq, LANES), lambda bi, hp, s, qi, kj: (bi, kj[s], 0)),
                  pl.BlockSpec((1, LANES, tq), lambda bi, hp, s, qi, kj: (bi, hp, kj[s]))],
        out_specs=pl.BlockSpec((1, tq, LANES), lambda bi, hp, s, qi, kj: (bi, qi[s], hp)),
        scratch_shapes=[pltpu.VMEM((2, 2 * LANES, tq), BF16), pltpu.VMEM((2, 1, tq), F32),
                        pltpu.VMEM((2, D_HEAD + ONES_ROWS, tq), F32)])
    return pl.pallas_call(
        _fox_kernel,
        grid_spec=grid_spec,
        out_shape=jax.ShapeDtypeStruct((b, t, c), BF16),
        compiler_params=_cparams(("arbitrary", "arbitrary", "arbitrary")),
        name="fox",
    )(qi, kj, qT, k, caug, vT)


def _fox_cache_kernel(q_ref, kn_ref, vn_ref, kp_ref, vp_ref, lfp_ref, lfn_ref, o_ref, cum_ref, *, blk):
    t = q_ref.shape[1]
    p_len = kp_ref.shape[1]
    c = q_ref.shape[2]
    n_cols = N_HEADS * t

    r = lax.broadcasted_iota(jnp.int32, (blk, blk), 0)
    cc = lax.broadcasted_iota(jnp.int32, (blk, blk), 1)
    lower = (cc <= r).astype(BF16)
    carry = jnp.zeros((1, N_HEADS), F32)
    for i in range(p_len // blk):
        cs = _dot_exact_lhs(lower, lfp_ref[0, i * blk:(i + 1) * blk, :]) + carry
        cum_ref[i * blk:(i + 1) * blk, :] = cs
        carry = cs[blk - 1:blk, :]
    cum_new = _dot_exact_lhs(lower[:t, :t], lfn_ref[0]) + carry

    eh = lax.broadcasted_iota(jnp.int32, (N_HEADS, n_cols), 0)
    ec = lax.broadcasted_iota(jnp.int32, (N_HEADS, n_cols), 1)
    expand = (ec // t == eh).astype(BF16)

    q = q_ref[0]
    qrep = jnp.concatenate([q] * N_HEADS, axis=0)
    qr = lax.broadcasted_iota(jnp.int32, (n_cols, c), 0)
    ql = lax.broadcasted_iota(jnp.int32, (n_cols, c), 1)
    qbd = jnp.where(qr // t == ql // D_HEAD, qrep, jnp.zeros_like(qrep))

    s_past = _dot_nt(kp_ref[0].astype(BF16), qbd) - _dot_exact_rhs(cum_ref[...], expand)
    s_new = _dot_nt(kn_ref[0], qbd) - _dot_exact_rhs(cum_new, expand)
    key = lax.broadcasted_iota(jnp.int32, (t, n_cols), 0)
    qry = lax.broadcasted_iota(jnp.int32, (t, n_cols), 1) % t
    s_new = jnp.where(key <= qry, s_new, -jnp.inf)

    m = jnp.maximum(jnp.max(s_past, axis=0, keepdims=True), jnp.max(s_new, axis=0, keepdims=True))
    p_past = jnp.exp(s_past - m)
    p_new = jnp.exp(s_new - m)
    inv = 1.0 / (jnp.sum(p_past, axis=0, keepdims=True) + jnp.sum(p_new, axis=0, keepdims=True))
    p_past = (p_past * inv).astype(BF16)
    p_new = (p_new * inv).astype(BF16)
    o_full = _dot_tn(p_past, vp_ref[0].astype(BF16)) + _dot_tn(p_new, vn_ref[0])
    for h in range(N_HEADS):
        o_ref[0, :, h * D_HEAD:(h + 1) * D_HEAD] = (
            o_full[h * t:(h + 1) * t, h * D_HEAD:(h + 1) * D_HEAD].astype(o_ref.dtype))


def _fox_cache(q, kn, vn, kp, vp, lf_past, lf_new):
    b, t, c = q.shape
    p_len = kp.shape[1]
    blk = 256
    assert p_len % blk == 0 and t <= blk
    new = lambda: pl.BlockSpec((1, t, c), lambda i: (i, 0, 0))
    past = lambda: pl.BlockSpec((1, p_len, c), lambda i: (i, 0, 0))
    return pl.pallas_call(
        functools.partial(_fox_cache_kernel, blk=blk),
        grid=(b,),
        in_specs=[new(), new(), new(), past(), past(),
                  pl.BlockSpec((1, p_len, N_HEADS), lambda i: (i, 0, 0)),
                  pl.BlockSpec((1, t, N_HEADS), lambda i: (i, 0, 0))],
        out_specs=new(),
        out_shape=jax.ShapeDtypeStruct((b, t, c), BF16),
        scratch_shapes=[pltpu.VMEM((p_len, N_HEADS), F32)],
        compiler_params=_cparams(("arbitrary",)),
        name="fox_cache",
    )(q, kn, vn, kp, vp, lf_past, lf_new)


def _merge_kernel(x_ref, g_ref, wg_ref, a_ref, b_ref, c_ref, wa_ref, wb_ref, wc_ref, wo_ref, o_ref):
    x = x_ref[...]
    d = x.shape[1]
    h = _rms_rows(x, g_ref[...]).astype(BF16)
    merged = jnp.zeros(x.shape, F32)
    for n, (br_ref, w_ref) in enumerate(((a_ref, wa_ref), (b_ref, wb_ref), (c_ref, wc_ref))):
        gate = _sigmoid(_dot(h, wg_ref[:, n * d:(n + 1) * d]))
        merged = merged + gate * _dot(br_ref[...], w_ref[...])
    o_ref[...] = x + _dot(merged.astype(BF16), wo_ref[...])


def _merge(x2d, g, w_gates, a, bm, c, wa, wb, wc, wo, *, tm):
    n, d = x2d.shape
    seg = a.shape[1]
    const = lambda shape: pl.BlockSpec(shape, lambda i: (0,) * len(shape))
    row = lambda width: pl.BlockSpec((tm, width), lambda i: (i, 0))
    return pl.pallas_call(
        _merge_kernel,
        grid=(n // tm,),
        in_specs=[row(d), const((1, d)), const(w_gates.shape), row(seg), row(seg), row(seg),
                  const(wa.shape), const(wb.shape), const(wc.shape), const(wo.shape)],
        out_specs=row(d),
        out_shape=jax.ShapeDtypeStruct((n, d), F32),
        compiler_params=_cparams(("arbitrary",)),
        name="merge",
    )(x2d, g, w_gates, a, bm, c, wa, wb, wc, wo)


def _erf(x):
    return lax.erf(x)


def _ffn_kernel(x_ref, g_ref, wup_ref, st_ref, wdw_ref, bdw_ref, wdn_ref, o_ref, buf_ref, abuf, *, n_chunks):
    tt, d = x_ref.shape[1], x_ref.shape[2]
    dff = wdn_ref.shape[0]
    cf = dff // n_chunks
    width = wdw_ref.shape[0]
    t = pl.program_id(1)
    x = x_ref[0]
    h = _rms_rows(x, g_ref[...]).astype(BF16)

    @pl.when(t == 0)
    def _():
        abuf[0:SUBLANES, :] = st_ref[0]

    @pl.when(t > 0)
    def _():
        abuf[0:SUBLANES, :] = abuf[tt:tt + SUBLANES, :]

    out = x
    first = SUBLANES - (width - 1)
    for n in range(n_chunks):
        cs = slice(n * cf, (n + 1) * cf)
        abuf[SUBLANES:SUBLANES + tt, cs] = _dot(h, wup_ref[:, cs])
        gate = _dot(h, wup_ref[:, dff + n * cf:dff + (n + 1) * cf])
        conv = jnp.zeros((tt, cf), F32) + bdw_ref[:, cs]
        for j in range(width):
            conv = conv + wdw_ref[j:j + 1, cs] * abuf[first + j:first + j + tt, cs]
        act = 0.5 * conv * (1.0 + _erf(conv * (2.0 ** -0.5))) * gate
        out = out + _dot(act.astype(BF16), wdn_ref[cs, :])
    o_ref[0] = out
    buf_ref[0] = abuf[SUBLANES + tt - (width - 1):SUBLANES + tt, :]


def _ffn(x, g, w_up, state, w_dw, b_dw, w_down, *, tt, n_chunks):
    b, t, d = x.shape
    dff = w_down.shape[0]
    width = w_dw.shape[0]
    st = jnp.pad(state.astype(F32), ((0, 0), (SUBLANES - (width - 1), 0), (0, 0)))
    const = lambda shape: pl.BlockSpec(shape, lambda i, j: (0,) * len(shape), pipeline_mode=pl.Buffered(1))
    return pl.pallas_call(
        functools.partial(_ffn_kernel, n_chunks=n_chunks),
        grid=(b, t // tt),
        in_specs=[pl.BlockSpec((1, tt, d), lambda i, j: (i, j, 0)),
                  const((1, d)), const(w_up.shape),
                  pl.BlockSpec((1, SUBLANES, dff), lambda i, j: (i, 0, 0)),
                  const(w_dw.shape), const((1, dff)), const(w_down.shape)],
        out_specs=[pl.BlockSpec((1, tt, d), lambda i, j: (i, j, 0)),
                   pl.BlockSpec((1, width - 1, dff), lambda i, j: (i, 0, 0))],
        out_shape=[jax.ShapeDtypeStruct((b, t, d), F32),
                   jax.ShapeDtypeStruct((b, width - 1, dff), F32)],
        scratch_shapes=[pltpu.VMEM((SUBLANES + tt, dff), F32)],
        compiler_params=_cparams(("arbitrary", "arbitrary")),
        name="ffn",
    )(x, g, w_up, st, w_dw, b_dw, w_down)


def _rope_tables(pos):
    half = D_HEAD // 2
    inv = jnp.exp(-math.log(ROPE_BASE) * jnp.arange(half, dtype=F32) / half)
    ang = pos.astype(F32)[:, None] * inv[None, :]
    cos, sin = jnp.cos(ang), jnp.sin(ang)
    cos_h = jnp.concatenate([cos, cos], axis=1)
    sin_h = jnp.concatenate([-sin, sin], axis=1)
    reps = LANES // D_HEAD
    return jnp.tile(cos_h, (1, reps)), jnp.tile(sin_h, (1, reps))


def _layer(x, pos, conv_state, ret_state, fox_past, ffn_state, lw, cfg):
    (g_mix, w_in, b_fgate, w_dw_conv, b_dw_conv, ln_conv_g, ln_conv_b, w_conv_out,
     gn_ret_g, w_ret_out, g_q_fox, g_k_fox, w_fox_out, w_out,
     g_ffn, w_ffn_up, w_ffn_dw, b_ffn_dw, w_ffn_down) = lw
    b, t, d = x.shape
    n_main = 9 * SEG
    row = lambda v: v.reshape(1, -1).astype(F32)

    w_main = w_in[:, :n_main].astype(BF16)
    w_ffT = jnp.transpose(w_in[:, n_main:n_main + N_HEADS]).astype(BF16)
    w_gates = w_in[:, n_main + N_HEADS:].astype(BF16)
    cos_t, sin_t = _rope_tables(pos)
    head_id = np.arange(SEG) // D_HEAD
    bd = jnp.asarray((head_id[:, None] == head_id[None, :]) / D_HEAD, BF16)
    tile_head = lambda v: jnp.tile(v.astype(F32), N_HEADS).reshape(1, SEG)

    transposed_ops = None
    if fox_past is None:
        gqT = jnp.broadcast_to(jnp.tile(g_q_fox.astype(F32), N_HEADS)[:, None], (SEG, LANES))
        transposed_ops = (jnp.transpose(w_in[:, 6 * SEG:7 * SEG]).astype(BF16),
                          jnp.transpose(w_in[:, 8 * SEG:9 * SEG]).astype(BF16), gqT)
    (u, rq, rk, rv, sg, fq, fk, fkb, fv, fvb), lf = _inproj(
        x, row(g_mix), w_main, w_ffT, b_fgate.reshape(N_HEADS, 1).astype(F32), cos_t, sin_t,
        tile_head(g_q_fox), tile_head(g_k_fox), bd, transposed_ops, bb=cfg["bb"], tt=cfg["tt"])

    a_br = _conv_branch(u, conv_state, w_dw_conv.astype(F32), row(b_dw_conv), row(ln_conv_g), row(ln_conv_b),
                        tt=cfg["tt_seq"])
    new_conv = u[:, t - (w_dw_conv.shape[0] - 1):, :]

    b_br, new_ret = _retention(rq, rk, rv, sg, ret_state.astype(F32), row(gn_ret_g), rows=cfg["ret_rows"])

    logf = jnp.transpose(lf, (1, 2, 0))
    if fox_past is None:
        c_br = _fox_prompt(fq, fkb, fvb, _cumaug(logf), tq=cfg["tq"])
    else:
        k_past, v_past, logf_past = fox_past
        p_len = k_past.shape[1]
        c_br = _fox_cache(fq, fkb, fvb, k_past.reshape(b, p_len, SEG), v_past.reshape(b, p_len, SEG),
                          logf_past.astype(F32), logf)

    n = b * t
    x1 = _merge(x.reshape(n, d), row(g_mix), w_gates, a_br.reshape(n, SEG), b_br.reshape(n, SEG),
                c_br.reshape(n, SEG), w_conv_out.astype(BF16), w_ret_out.astype(BF16),
                w_fox_out.astype(BF16), w_out.astype(BF16), tm=cfg["tm"]).reshape(b, t, d)

    x2, new_ffn = _ffn(x1, row(g_ffn), w_ffn_up.astype(BF16), ffn_state, w_ffn_dw.astype(F32), row(b_ffn_dw),
                       w_ffn_down.astype(BF16), tt=cfg["tt_seq"], n_chunks=cfg["ffn_chunks"])

    states = (new_conv, new_ret, fk.reshape(b, t, N_HEADS, D_HEAD), fv.reshape(b, t, N_HEADS, D_HEAD),
              logf, new_ffn)
    return x2, states


def _group_cfg(b, t):
    if t >= 512:
        return dict(bb=1, tt=512, tt_seq=512, ret_rows=256, tq=512, tm=512, ffn_chunks=2)
    bb = max(1, min(b, 512 // t))
    return dict(bb=bb, tt=t, tt_seq=t, ret_rows=t, tq=t, tm=min(b * t, 512), ffn_chunks=2)


def kernel(x_prompt, x_sample, state_conv, state_ret, cache_fox_k, cache_fox_v, cache_fox_logf, state_ffn_conv, g_mix, w_in, b_fgate, w_dw_conv, b_dw_conv, ln_conv_g, ln_conv_b, w_conv_out, gn_ret_g, w_ret_out, g_q_fox, g_k_fox, w_fox_out, w_out, g_ffn, w_ffn_up, w_ffn_dw, b_ffn_dw, w_ffn_down):
    bp, sp, _ = x_prompt.shape
    bs, ts, _ = x_sample.shape
    depth = w_in.shape[0]
    p_len = cache_fox_k.shape[2]
    conv_w = w_dw_conv.shape[1]
    ffn_w = w_ffn_dw.shape[1]
    d_conv = w_dw_conv.shape[2]
    d_ff = w_ffn_down.shape[1]
    assert sp >= conv_w - 1 and ts >= conv_w - 1 and conv_w - 1 <= HALO and ffn_w - 1 <= SUBLANES
    assert d_conv == SEG and w_in.shape[2] == 9 * SEG + N_HEADS + 3 * x_prompt.shape[2]
    pos_p = jnp.arange(sp)
    pos_s = p_len + jnp.arange(ts)
    cfg_p = _group_cfg(bp, sp)
    cfg_s = _group_cfg(bs, ts)
    weights = (g_mix, w_in, b_fgate, w_dw_conv, b_dw_conv, ln_conv_g, ln_conv_b, w_conv_out,
               gn_ret_g, w_ret_out, g_q_fox, g_k_fox, w_fox_out, w_out,
               g_ffn, w_ffn_up, w_ffn_dw, b_ffn_dw, w_ffn_down)
    xp, xs = x_prompt, x_sample
    st_p, st_s = [], []
    for l in range(depth):
        lw = tuple(w[l] for w in weights)
        xp, sp_l = _layer(xp, pos_p,
                          jnp.zeros((bp, conv_w - 1, d_conv), F32),
                          jnp.zeros((bp, N_HEADS, D_HEAD, D_HEAD), F32),
                          None,
                          jnp.zeros((bp, ffn_w - 1, d_ff), F32), lw, cfg_p)
        xs, ss_l = _layer(xs, pos_s, state_conv[l], state_ret[l],
                          (cache_fox_k[l], cache_fox_v[l], cache_fox_logf[l]),
                          state_ffn_conv[l], lw, cfg_s)
        st_p.append(sp_l)
        st_s.append(ss_l)
    stack = lambda sts: tuple(jnp.stack([s[i] for s in sts]) for i in range(6))
    return (xp, xs) + stack(st_p) + stack(st_s)
```

```python
import functools
import math

import jax
import jax.numpy as jnp
import numpy as np
from jax import lax
from jax.experimental import pallas as pl
from jax.experimental.pallas import tpu as pltpu

F32 = jnp.float32
BF16 = jnp.bfloat16

EPS = 1e-6
ROPE_BASE = 10000.0
CHUNK = 64
N_HEADS = 8
D_HEAD = 64
SEG = N_HEADS * D_HEAD
LANES = 128
SUBLANES = 8
HALO = 32
VMEM_LIMIT = 56 * 1024 * 1024
LOG2E = math.log2(math.e)


def _cparams(sem):
    return pltpu.CompilerParams(dimension_semantics=sem, vmem_limit_bytes=VMEM_LIMIT)


def _dot(a, b):
    return jnp.dot(a, b, preferred_element_type=F32)


def _dot_nt(a, b):
    return lax.dot_general(a, b, (((1,), (1,)), ((), ())), preferred_element_type=F32)


def _dot_tn(a, b):
    return lax.dot_general(a, b, (((0,), (0,)), ((), ())), preferred_element_type=F32)


def _split3(x):
    hi = x.astype(BF16)
    r1 = x - hi.astype(F32)
    mid = r1.astype(BF16)
    lo = (r1 - mid.astype(F32)).astype(BF16)
    return hi, mid, lo


def _dot_exact_rhs(a_f32, b_bf16):
    hi, mid, lo = _split3(a_f32)
    return _dot(hi, b_bf16) + _dot(mid, b_bf16) + _dot(lo, b_bf16)


def _dot_exact_lhs(a_bf16, b_f32):
    hi, mid, lo = _split3(b_f32)
    return _dot(a_bf16, hi) + _dot(a_bf16, mid) + _dot(a_bf16, lo)


def _sigmoid(x):
    return 1.0 / (1.0 + jnp.exp(-x))


def _silu(x):
    return x * _sigmoid(x)


def _rms_rows(x, g):
    return x * lax.rsqrt(jnp.mean(x * x, axis=-1, keepdims=True) + EPS) * g


def _inproj_common(x_ref, g_ref, w_ref, cos_ref, sin_ref, u_ref, rq_ref, rk_ref, rv_ref, sg_ref):
    bb, tt, d = x_ref.shape
    tm = bb * tt
    x = x_ref[...].reshape(tm, d)
    h = _rms_rows(x, g_ref[...]).astype(BF16)

    def proj(i):
        return _dot(h, w_ref[:, i * SEG:(i + 1) * SEG])

    def put(ref, val):
        ref[...] = val.reshape(bb, tt, SEG).astype(ref.dtype)

    put(u_ref, proj(0) * _sigmoid(proj(1)))

    cos = cos_ref[...][None]
    sin = sin_ref[...][None]
    half = D_HEAD // 2
    lane = lax.broadcasted_iota(jnp.int32, (1, LANES), 1)
    first_half = (lane % D_HEAD) < half

    def rotary(p, scale):
        cols = []
        for c in range(SEG // LANES):
            pc = p[:, c * LANES:(c + 1) * LANES]
            partner = jnp.where(first_half, pltpu.roll(pc, LANES - half, 1), pltpu.roll(pc, half, 1))
            r = pc.reshape(bb, tt, LANES) * cos + partner.reshape(bb, tt, LANES) * sin
            cols.append(r * scale)
        return jnp.concatenate(cols, axis=-1)

    rq_ref[...] = rotary(proj(2), 1.0).astype(BF16)
    rk_ref[...] = rotary(proj(3), D_HEAD ** -0.5).astype(BF16)
    put(rv_ref, proj(4))
    put(sg_ref, _silu(proj(5)))
    return h, proj, put


def _head_rms(p, g, bd):
    ms = _dot((p * p).astype(BF16), bd)
    return p * lax.rsqrt(ms + EPS) * g


def _head_rms_t(z, gT_ref):
    tm = z.shape[1]
    z3 = z.reshape(N_HEADS, D_HEAD, tm)
    ms = jnp.mean(z3 * z3, axis=1, keepdims=True)
    g = jnp.concatenate([gT_ref[...]] * (tm // LANES), axis=1).reshape(N_HEADS, D_HEAD, tm)
    return (z3 * lax.rsqrt(ms + EPS) * g).reshape(SEG, tm)


def _log_forget(wff_ref, bff_ref, h):
    z = _dot_nt(wff_ref[...], h) + bff_ref[...]
    return jnp.minimum(z, 0.0) - jnp.log(1.0 + jnp.exp(-jnp.abs(z)))


def _inproj_seq_kernel(x_ref, g_ref, w_ref, wff_ref, bff_ref, cos_ref, sin_ref, gk_ref, bd_ref,
                       wT_ref, gqT_ref, gkT_ref, *rest):
    (u_ref, rq_ref, rk_ref, rv_ref, sg_ref, fqT_ref, fkb_ref, fvTb_ref, fkT_ref, fvT_ref, lf_ref) = rest[-11:]
    h, proj, put = _inproj_common(x_ref, g_ref, w_ref, cos_ref, sin_ref, u_ref, rq_ref, rk_ref, rv_ref, sg_ref)
    put(fkb_ref, _head_rms(proj(7), gk_ref[...], bd_ref[...]))
    zq = _dot_nt(wT_ref[0:SEG, :], h)
    fqT_ref[0] = (_head_rms_t(zq, gqT_ref) * (D_HEAD ** -0.5 * LOG2E)).astype(BF16)
    zk = _dot_nt(wT_ref[SEG:2 * SEG, :], h)
    fkT_ref[0, 0] = _head_rms_t(zk, gkT_ref)
    fvT = _dot_nt(wT_ref[2 * SEG:3 * SEG, :], h)
    fvT_ref[0, 0] = fvT
    fvTb_ref[0] = fvT.astype(BF16)
    lf_ref[0] = _log_forget(wff_ref, bff_ref, h)


def _inproj_rows_kernel(x_ref, g_ref, w_ref, wff_ref, bff_ref, cos_ref, sin_ref, gq_ref, gk_ref, bd_ref,
                        u_ref, rq_ref, rk_ref, rv_ref, sg_ref, fq_ref, fk_ref, fkb_ref, fv_ref, fvb_ref, lf_ref):
    h, proj, put = _inproj_common(x_ref, g_ref, w_ref, cos_ref, sin_ref, u_ref, rq_ref, rk_ref, rv_ref, sg_ref)
    bd = bd_ref[...]
    put(fq_ref, _head_rms(proj(6), gq_ref[...], bd) * (D_HEAD ** -0.5))
    fk = _head_rms(proj(7), gk_ref[...], bd)
    put(fk_ref, fk)
    put(fkb_ref, fk)
    fv = proj(8)
    put(fv_ref, fv)
    put(fvb_ref, fv)
    lf_ref[0] = _log_forget(wff_ref, bff_ref, h)


def _inproj_seq(x, g, w_main, w_ffT, b_ff, cos_t, sin_t, gk, bd, wT, gqT, gkT, stacked, layer, depth, *, tt):
    b, t, d = x.shape
    act = lambda: pl.BlockSpec((1, tt, SEG), lambda i, j: (j, i, 0))
    act_t = lambda: pl.BlockSpec((1, SEG, tt), lambda i, j: (j, 0, i))
    stk = lambda: pl.BlockSpec((1, 1, SEG, tt), lambda i, j: (layer, j, 0, i))
    const = lambda shape: pl.BlockSpec(shape, lambda i, j: (0,) * len(shape))
    nat = lambda dt: jax.ShapeDtypeStruct((b, t, SEG), dt)
    tr = jax.ShapeDtypeStruct((b, SEG, t), BF16)
    stacked_shape = jax.ShapeDtypeStruct((depth, b, SEG, t), F32)
    out_shape = [nat(F32), nat(BF16), nat(BF16), nat(BF16), nat(BF16),
                 tr, nat(BF16), tr,
                 stacked_shape, stacked_shape,
                 jax.ShapeDtypeStruct((b, N_HEADS, t), F32)]
    out_specs = [act(), act(), act(), act(), act(), act_t(), act(), act_t(), stk(), stk(),
                 pl.BlockSpec((1, N_HEADS, tt), lambda i, j: (j, 0, i))]
    operands = [x, g, w_main, w_ffT, b_ff, cos_t, sin_t, gk, bd, wT, gqT, gkT]
    in_specs = [pl.BlockSpec((1, tt, d), lambda i, j: (j, i, 0)),
                const((1, d)), const(w_main.shape), const(w_ffT.shape), const((N_HEADS, 1)),
                pl.BlockSpec((tt, LANES), lambda i, j: (i, 0)), pl.BlockSpec((tt, LANES), lambda i, j: (i, 0)),
                const((1, SEG)), const((SEG, SEG)), const(wT.shape), const(gqT.shape), const(gkT.shape)]
    aliases = {}
    if stacked is not None:
        aliases = {len(operands): 8, len(operands) + 1: 9}
        operands += list(stacked)
        in_specs += [pl.BlockSpec(memory_space=pl.ANY)] * 2
    return pl.pallas_call(
        _inproj_seq_kernel,
        grid=(t // tt, b),
        in_specs=in_specs,
        out_specs=out_specs,
        out_shape=out_shape,
        input_output_aliases=aliases,
        compiler_params=_cparams(("arbitrary", "arbitrary")),
        name="inproj_seq",
    )(*operands)


def _inproj_rows(x, g, w_main, w_ffT, b_ff, cos_t, sin_t, gq, gk, bd, *, bb, tt):
    b, t, d = x.shape
    n_t, n_b = t // tt, b // bb
    tm = bb * tt
    act = lambda: pl.BlockSpec((bb, tt, SEG), lambda i, j: (j, i, 0))
    const = lambda shape: pl.BlockSpec(shape, lambda i, j: (0,) * len(shape))
    nat = lambda dt: jax.ShapeDtypeStruct((b, t, SEG), dt)
    out_shape = [nat(F32), nat(BF16), nat(BF16), nat(BF16), nat(BF16), nat(BF16),
                 nat(F32), nat(BF16), nat(F32), nat(BF16),
                 jax.ShapeDtypeStruct((n_t * n_b, N_HEADS, tm), F32)]
    out_specs = [act() for _ in range(10)] + [pl.BlockSpec((1, N_HEADS, tm), lambda i, j: (i * n_b + j, 0, 0))]
    outs = pl.pallas_call(
        _inproj_rows_kernel,
        grid=(n_t, n_b),
        in_specs=[pl.BlockSpec((bb, tt, d), lambda i, j: (j, i, 0)),
                  const((1, d)), const(w_main.shape), const(w_ffT.shape), const((N_HEADS, 1)),
                  pl.BlockSpec((tt, LANES), lambda i, j: (i, 0)), pl.BlockSpec((tt, LANES), lambda i, j: (i, 0)),
                  const((1, SEG)), const((1, SEG)), const((SEG, SEG))],
        out_specs=out_specs,
        out_shape=out_shape,
        compiler_params=_cparams(("arbitrary", "arbitrary")),
        name="inproj_rows",
    )(x, g, w_main, w_ffT, b_ff, cos_t, sin_t, gq, gk, bd)
    lf = outs[-1].reshape(n_t, n_b, N_HEADS, bb, tt)
    lf = jnp.transpose(lf, (1, 3, 2, 0, 4)).reshape(b, N_HEADS, t)
    return outs[:-1], lf


def _conv_kernel(u_ref, st_ref, w_ref, b_ref, lg_ref, lb_ref, a_ref, xbuf, xsh, *, width, rows):
    tt = u_ref.shape[1]
    t = pl.program_id(1)
    n_rows = HALO + tt

    @pl.when(t == 0)
    def _():
        xbuf[0:HALO, :] = st_ref[0]

    @pl.when(t > 0)
    def _():
        xbuf[0:HALO, :] = xbuf[tt:tt + HALO, :]

    xbuf[HALO:n_rows, :] = u_ref[0]
    xbuf[n_rows:n_rows + SUBLANES, :] = jnp.zeros((SUBLANES, xbuf.shape[1]), F32)
    for s0 in range(0, n_rows, rows):
        win = xbuf[s0:s0 + rows + SUBLANES, :]
        for sh in range(1, SUBLANES):
            xsh[sh - 1, s0:s0 + rows, :] = win[sh:sh + rows, :]
    first = HALO - (width - 1)
    bias = b_ref[...]
    lg = lg_ref[...]
    lb = lb_ref[...]
    for r0 in range(0, tt, rows):
        acc = jnp.zeros((rows, xbuf.shape[1]), F32) + bias
        for j in range(width):
            whole, sh = divmod(first + j, SUBLANES)
            lo = r0 + whole * SUBLANES
            tap = xbuf[lo:lo + rows, :] if sh == 0 else xsh[sh - 1, lo:lo + rows, :]
            acc = acc + w_ref[j:j + 1, :] * tap
        mu = jnp.mean(acc, axis=-1, keepdims=True)
        xc = acc - mu
        var = jnp.mean(xc * xc, axis=-1, keepdims=True)
        y = xc * lax.rsqrt(var + EPS) * lg + lb
        a_ref[0, r0:r0 + rows, :] = _silu(y).astype(a_ref.dtype)


def _conv_branch(u, state, w_dw, b_dw, ln_g, ln_b, *, tt):
    b, t, c = u.shape
    width = w_dw.shape[0]
    st = jnp.pad(state.astype(F32), ((0, 0), (HALO - (width - 1), 0), (0, 0)))
    rows = min(tt, 32)
    const = lambda shape: pl.BlockSpec(shape, lambda i, j: (0,) * len(shape))
    return pl.pallas_call(
        functools.partial(_conv_kernel, width=width, rows=rows),
        grid=(b, t // tt),
        in_specs=[pl.BlockSpec((1, tt, c), lambda i, j: (i, j, 0)),
                  pl.BlockSpec((1, HALO, c), lambda i, j: (i, 0, 0)),
                  const((width, c)), const((1, c)), const((1, c)), const((1, c))],
        out_specs=pl.BlockSpec((1, tt, c), lambda i, j: (i, j, 0)),
        out_shape=jax.ShapeDtypeStruct((b, t, c), BF16),
        scratch_shapes=[pltpu.VMEM((HALO + tt + SUBLANES, c), F32),
                        pltpu.VMEM((SUBLANES - 1, HALO + tt, c), F32)],
        compiler_params=_cparams(("arbitrary", "arbitrary")),
        name="conv",
    )(u, st, w_dw, b_dw, ln_g, ln_b)


def _ret_kernel(q_ref, k_ref, v_ref, sg_ref, st0_ref, dmat_ref, qdec_ref, kdec_ref, gn_ref, bd_ref,
                o_ref, st_ref, oall, *, tile_decay):
    t = pl.program_id(1)

    @pl.when(t == 0)
    def _():
        st_ref[...] = st0_ref[...]

    q = q_ref[0]
    k = k_ref[0]
    q_in = (q.astype(F32) * qdec_ref[...]).astype(BF16)
    k_end = (k.astype(F32) * kdec_ref[...]).astype(BF16)
    for h in range(N_HEADS):
        sl = slice(h * D_HEAD, (h + 1) * D_HEAD)
        qh, kh, vh = q[:, sl], k[:, sl], v_ref[0, :, sl]
        s = _dot_nt(qh, kh) * dmat_ref[h]
        state = st_ref[0, h]
        o = _dot(s.astype(BF16), vh) + _dot(q_in[:, sl], state.astype(BF16))
        st_ref[0, h] = state * tile_decay[h] + _dot_tn(k_end[:, sl], vh)
        oall[:, sl] = o
    o = oall[...]
    bd = bd_ref[...]
    o_hi = o.astype(BF16)
    o_lo = (o - o_hi.astype(F32)).astype(BF16)
    oc = o - (_dot(o_hi, bd) + _dot(o_lo, bd))
    var = _dot((oc * oc).astype(BF16), bd)
    o_ref[0] = (oc * lax.rsqrt(var + EPS) * gn_ref[...] * sg_ref[0].astype(F32)).astype(o_ref.dtype)


def _ret_tables(rows, chunk):
    lg = np.log1p(-np.exp2(-5.0 - np.arange(N_HEADS, dtype=np.float64)))
    idx = np.arange(rows, dtype=np.float64)
    dist = idx[:, None] - idx[None, :]
    visible = (idx[None, :] // chunk) <= (idx[:, None] // chunk)
    dmat = np.where(visible[None], np.exp(lg[:, None, None] * np.abs(dist)[None]), 0.0)
    qdec = np.repeat(np.exp(lg[None, :] * (idx[:, None] + 1.0)), D_HEAD, axis=1)
    kdec = np.repeat(np.exp(lg[None, :] * (rows - 1.0 - idx[:, None])), D_HEAD, axis=1)
    tile_decay = tuple(float(v) for v in np.exp(lg * rows))
    return (jnp.asarray(dmat, F32), jnp.asarray(qdec, F32), jnp.asarray(kdec, F32), tile_decay)


def _retention(q, k, v, sg, state0, gn_g, bd, *, rows):
    b, t, c = q.shape
    chunk = min(t, CHUNK)
    dmat, qdec, kdec, tile_decay = _ret_tables(rows, chunk)
    const = lambda shape: pl.BlockSpec(shape, lambda i, j: (0,) * len(shape))
    act = lambda: pl.BlockSpec((1, rows, c), lambda i, j: (i, j, 0))
    st_spec = lambda: pl.BlockSpec((1, N_HEADS, D_HEAD, D_HEAD), lambda i, j: (i, 0, 0, 0))
    return pl.pallas_call(
        functools.partial(_ret_kernel, tile_decay=tile_decay),
        grid=(b, t // rows),
        in_specs=[act(), act(), act(), act(), st_spec(),
                  const(dmat.shape), const(qdec.shape), const(kdec.shape), const((1, c)), const(bd.shape)],
        out_specs=[act(), st_spec()],
        out_shape=[jax.ShapeDtypeStruct((b, t, c), BF16),
                   jax.ShapeDtypeStruct((b, N_HEADS, D_HEAD, D_HEAD), F32)],
        scratch_shapes=[pltpu.VMEM((rows, c), F32)],
        compiler_params=_cparams(("arbitrary", "arbitrary")),
        name="ret",
    )(q, k, v, sg, state0, dmat, qdec, kdec, gn_g, bd)


CUM_BLOCK = 256
AUG_PIECES = 3


def _cumaug_kernel(lf_ref, o_ref):
    t = lf_ref.shape[1]
    blk = min(CUM_BLOCK, t)
    r = lax.broadcasted_iota(jnp.int32, (blk, blk), 0)
    c = lax.broadcasted_iota(jnp.int32, (blk, blk), 1)
    lower = (c <= r).astype(BF16)
    sh = lax.broadcasted_iota(jnp.int32, (N_HEADS, LANES), 0)
    sl = lax.broadcasted_iota(jnp.int32, (N_HEADS, LANES), 1)
    place = [(sl == AUG_PIECES * sh + p).astype(BF16) for p in range(AUG_PIECES)]
    carry = jnp.zeros((1, N_HEADS), F32)
    for i in range(t // blk):
        cs = _dot_exact_lhs(lower, lf_ref[0, i * blk:(i + 1) * blk, :]) + carry
        carry = cs[blk - 1:blk, :]
        pieces = _split3(-LOG2E * cs)
        aug = sum(_dot(piece, sel) for piece, sel in zip(pieces, place))
        o_ref[0, i * blk:(i + 1) * blk, :] = aug.astype(o_ref.dtype)


def _cumaug(logf):
    b, t, h = logf.shape
    return pl.pallas_call(
        _cumaug_kernel,
        grid=(b,),
        in_specs=[pl.BlockSpec((1, t, h), lambda i: (i, 0, 0))],
        out_specs=pl.BlockSpec((1, t, LANES), lambda i: (i, 0, 0)),
        out_shape=jax.ShapeDtypeStruct((b, t, LANES), BF16),
        compiler_params=_cparams(("arbitrary",)),
        name="cumaug",
    )(logf)


ONES_ROWS = 16


def _fox_kernel(qi_ref, kj_ref, qT_ref, k_ref, ca_ref, vT_ref, o_ref, qaug, m_ref, acc_ref, s0_ref, s1_ref):
    s = pl.program_id(1)
    i = qi_ref[s]
    j = kj_ref[s]
    tq = qT_ref.shape[2]
    tk = k_ref.shape[1]
    s_bufs = (s0_ref, s1_ref)

    @pl.when(j == 0)
    def _():
        row = lax.broadcasted_iota(jnp.int32, (LANES, tq), 0)
        for h in range(N_HEADS):
            qT = qT_ref[0, (h // 2) * LANES:(h // 2 + 1) * LANES, :]
            qaug[h, 0:LANES, :] = jnp.where(row // D_HEAD == h % 2, qT, jnp.zeros_like(qT))
            pick = (row >= AUG_PIECES * h) & (row < AUG_PIECES * (h + 1))
            qaug[h, LANES:2 * LANES, :] = jnp.where(pick, 1.0, 0.0).astype(BF16)
        m_ref[...] = jnp.full(m_ref.shape, -jnp.inf, F32)
        acc_ref[...] = jnp.zeros(acc_ref.shape, F32)

    def step(diagonal):
        ca = ca_ref[0]
        ones = jnp.ones((ONES_ROWS, tk), BF16)

        def logits(h):
            kk = jnp.concatenate([k_ref[0, :, (h // 2) * LANES:(h // 2 + 1) * LANES], ca], axis=1)
            s_bufs[h % 2][...] = _dot(kk, qaug[h])

        logits(0)
        for h in range(N_HEADS):
            if h + 1 < N_HEADS:
                logits(h + 1)
            sT = s_bufs[h % 2][...]
            if diagonal:
                key = lax.broadcasted_iota(jnp.int32, (tk, tq), 0)
                qry = lax.broadcasted_iota(jnp.int32, (tk, tq), 1)
                sT = jnp.where(key <= qry, sT, -jnp.inf)
            m_old = m_ref[h]
            m_new = jnp.maximum(m_old, jnp.max(sT, axis=0, keepdims=True))
            alpha = jnp.exp2(m_old - m_new)
            p = jnp.exp2(sT - m_new).astype(BF16)
            lhs = jnp.concatenate([vT_ref[0, h * D_HEAD:(h + 1) * D_HEAD, :], ones], axis=0)
            acc_ref[h] = alpha * acc_ref[h] + _dot(lhs, p)
            m_ref[h] = m_new

    @pl.when(j < i)
    def _():
        step(False)

    @pl.when(j == i)
    def _():
        step(True)
        for hp in range(N_HEADS // 2):
            outs = []
            for h in (2 * hp, 2 * hp + 1):
                acc = acc_ref[h]
                outs.append(acc[0:D_HEAD, :] / acc[D_HEAD:D_HEAD + 1, :])
            o_ref[0, :, hp * LANES:(hp + 1) * LANES] = (
                jnp.transpose(jnp.concatenate(outs, axis=0)).astype(o_ref.dtype))


def _fox_prompt(qT, k, vT, caug, *, tq):
    b, c, t = qT.shape
    n_q = t // tq
    pairs = [(i, j) for i in range(n_q) for j in range(i + 1)]
    qi = jnp.asarray([p[0] for p in pairs], jnp.int32)
    kj = jnp.asarray([p[1] for p in pairs], jnp.int32)
    grid_spec = pltpu.PrefetchScalarGridSpec(
        num_scalar_prefetch=2,
        grid=(b, len(pairs)),
        in_specs=[pl.BlockSpec((1, c, tq), lambda bi, s, qi, kj: (bi, 0, qi[s])),
                  pl.BlockSpec((1, tq, c), lambda bi, s, qi, kj: (bi, kj[s], 0)),
                  pl.BlockSpec((1, tq, LANES), lambda bi, s, qi, kj: (bi, kj[s], 0)),
                  pl.BlockSpec((1, c, tq), lambda bi, s, qi, kj: (bi, 0, kj[s]))],
        out_specs=pl.BlockSpec((1, tq, c), lambda bi, s, qi, kj: (bi, qi[s], 0)),
        scratch_shapes=[pltpu.VMEM((N_HEADS, 2 * LANES, tq), BF16), pltpu.VMEM((N_HEADS, 1, tq), F32),
                        pltpu.VMEM((N_HEADS, D_HEAD + ONES_ROWS, tq), F32),
                        pltpu.VMEM((tq, tq), F32), pltpu.VMEM((tq, tq), F32)])
    return pl.pallas_call(
        _fox_kernel,
        grid_spec=grid_spec,
        out_shape=jax.ShapeDtypeStruct((b, t, c), BF16),
        compiler_params=_cparams(("arbitrary", "arbitrary")),
        name="fox",
    )(qi, kj, qT, k, caug, vT)


def _fox_cache_kernel(q_ref, kn_ref, vn_ref, kpT_ref, vpT_ref, lfp_ref, lfn_ref, o_ref, cum_ref):
    t = q_ref.shape[1]
    p_len = kpT_ref.shape[2]
    c = q_ref.shape[2]
    n_rows = N_HEADS * t

    r = lax.broadcasted_iota(jnp.int32, (LANES, LANES), 0)
    cc = lax.broadcasted_iota(jnp.int32, (LANES, LANES), 1)
    upper = (r <= cc).astype(BF16)
    carry = jnp.zeros((N_HEADS, 1), F32)
    for i in range(p_len // LANES):
        cs = _dot_exact_rhs(lfp_ref[0, :, i * LANES:(i + 1) * LANES], upper) + carry
        cum_ref[:, i * LANES:(i + 1) * LANES] = cs
        carry = cs[:, LANES - 1:LANES]
    cum_new = _dot_exact_rhs(lfn_ref[0], upper[:t, :t]) + carry

    def per_query_rows(cum):
        n = cum.shape[1]
        return jnp.broadcast_to(cum[:, None, :], (N_HEADS, t, n)).reshape(n_rows, n)

    q = q_ref[0]
    qrep = jnp.concatenate([q] * N_HEADS, axis=0)
    qr = lax.broadcasted_iota(jnp.int32, (n_rows, c), 0)
    ql = lax.broadcasted_iota(jnp.int32, (n_rows, c), 1)
    qbd = jnp.where(qr // t == ql // D_HEAD, qrep, jnp.zeros_like(qrep))

    s_past = _dot(qbd, kpT_ref[0].astype(BF16)) - per_query_rows(cum_ref[...])
    s_new = _dot_nt(qbd, kn_ref[0]) - per_query_rows(cum_new)
    qry = lax.broadcasted_iota(jnp.int32, (n_rows, t), 0) % t
    key = lax.broadcasted_iota(jnp.int32, (n_rows, t), 1)
    s_new = jnp.where(key <= qry, s_new, -jnp.inf)

    m = jnp.maximum(jnp.max(s_past, axis=1, keepdims=True), jnp.max(s_new, axis=1, keepdims=True))
    p_past = jnp.exp(s_past - m)
    p_new = jnp.exp(s_new - m)
    inv = 1.0 / (jnp.sum(p_past, axis=1, keepdims=True) + jnp.sum(p_new, axis=1, keepdims=True))
    o_full = (_dot_nt((p_past * inv).astype(BF16), vpT_ref[0].astype(BF16))
              + _dot((p_new * inv).astype(BF16), vn_ref[0]))
    for h in range(N_HEADS):
        o_ref[0, :, h * D_HEAD:(h + 1) * D_HEAD] = (
            o_full[h * t:(h + 1) * t, h * D_HEAD:(h + 1) * D_HEAD].astype(o_ref.dtype))


def _fox_cache(q, kn, vn, kpT, vpT, lf_pastT, lf_newT):
    b, t, c = q.shape
    p_len = kpT.shape[2]
    assert p_len % LANES == 0 and t <= LANES
    new = lambda: pl.BlockSpec((1, t, c), lambda i: (i, 0, 0))
    past = lambda: pl.BlockSpec((1, c, p_len), lambda i: (i, 0, 0))
    return pl.pallas_call(
        _fox_cache_kernel,
        grid=(b,),
        in_specs=[new(), new(), new(), past(), past(),
                  pl.BlockSpec((1, N_HEADS, p_len), lambda i: (i, 0, 0)),
                  pl.BlockSpec((1, N_HEADS, t), lambda i: (i, 0, 0))],
        out_specs=new(),
        out_shape=jax.ShapeDtypeStruct((b, t, c), BF16),
        scratch_shapes=[pltpu.VMEM((N_HEADS, p_len), F32)],
        compiler_params=_cparams(("arbitrary",)),
        name="fox_cache",
    )(q, kn, vn, kpT, vpT, lf_pastT, lf_newT)


def _merge_kernel(x_ref, g_ref, wg_ref, a_ref, b_ref, c_ref, wa_ref, wb_ref, wc_ref, wo_ref, o_ref):
    x = x_ref[...]
    d = x.shape[1]
    h = _rms_rows(x, g_ref[...]).astype(BF16)
    merged = jnp.zeros(x.shape, F32)
    for n, (br_ref, w_ref) in enumerate(((a_ref, wa_ref), (b_ref, wb_ref), (c_ref, wc_ref))):
        gate = _sigmoid(_dot(h, wg_ref[:, n * d:(n + 1) * d]))
        merged = merged + gate * _dot(br_ref[...], w_ref[...])
    o_ref[...] = x + _dot(merged.astype(BF16), wo_ref[...])


def _merge(x2d, g, w_gates, a, bm, c, wa, wb, wc, wo, *, tm):
    n, d = x2d.shape
    seg = a.shape[1]
    const = lambda shape: pl.BlockSpec(shape, lambda i: (0,) * len(shape))
    row = lambda width: pl.BlockSpec((tm, width), lambda i: (i, 0))
    return pl.pallas_call(
        _merge_kernel,
        grid=(n // tm,),
        in_specs=[row(d), const((1, d)), const(w_gates.shape), row(seg), row(seg), row(seg),
                  const(wa.shape), const(wb.shape), const(wc.shape), const(wo.shape)],
        out_specs=row(d),
        out_shape=jax.ShapeDtypeStruct((n, d), F32),
        compiler_params=_cparams(("arbitrary",)),
        name="merge",
    )(x2d, g, w_gates, a, bm, c, wa, wb, wc, wo)


def _erf(x):
    return lax.erf(x)


def _ffn_kernel(x_ref, g_ref, wup_ref, st_ref, wdw_ref, bdw_ref, wdn_ref, o_ref, buf_ref, abuf, *, n_chunks):
    bb, tt, d = x_ref.shape
    tm = bb * tt
    dff = wdn_ref.shape[0]
    cf = dff // n_chunks
    width = wdw_ref.shape[0]
    t = pl.program_id(1)
    x = x_ref[...].reshape(tm, d)
    h = _rms_rows(x, g_ref[...]).astype(BF16)

    @pl.when(t == 0)
    def _():
        abuf[:, 0:SUBLANES, :] = st_ref[...]

    @pl.when(t > 0)
    def _():
        abuf[:, 0:SUBLANES, :] = abuf[:, tt:tt + SUBLANES, :]

    out = x
    first = SUBLANES - (width - 1)
    for n in range(n_chunks):
        cs = slice(n * cf, (n + 1) * cf)
        abuf[:, SUBLANES:SUBLANES + tt, cs] = _dot(h, wup_ref[:, cs]).reshape(bb, tt, cf)
        gate = _dot(h, wup_ref[:, dff + n * cf:dff + (n + 1) * cf])
        conv = jnp.zeros((bb, tt, cf), F32) + bdw_ref[:, cs]
        for j in range(width):
            conv = conv + wdw_ref[j:j + 1, cs] * abuf[:, first + j:first + j + tt, cs]
        conv = conv.reshape(tm, cf)
        act = 0.5 * conv * (1.0 + _erf(conv * (2.0 ** -0.5))) * gate
        out = out + _dot(act.astype(BF16), wdn_ref[cs, :])
    o_ref[...] = out.reshape(bb, tt, d)
    buf_ref[...] = abuf[:, SUBLANES + tt - (width - 1):SUBLANES + tt, :]


def _ffn(x, g, w_up, state, w_dw, b_dw, w_down, *, bb, tt, n_chunks):
    b, t, d = x.shape
    dff = w_down.shape[0]
    width = w_dw.shape[0]
    st = jnp.pad(state.astype(F32), ((0, 0), (SUBLANES - (width - 1), 0), (0, 0)))
    const = lambda shape: pl.BlockSpec(shape, lambda i, j: (0,) * len(shape), pipeline_mode=pl.Buffered(1))
    return pl.pallas_call(
        functools.partial(_ffn_kernel, n_chunks=n_chunks),
        grid=(b // bb, t // tt),
        in_specs=[pl.BlockSpec((bb, tt, d), lambda i, j: (i, j, 0)),
                  const((1, d)), const(w_up.shape),
                  pl.BlockSpec((bb, SUBLANES, dff), lambda i, j: (i, 0, 0)),
                  const(w_dw.shape), const((1, dff)), const(w_down.shape)],
        out_specs=[pl.BlockSpec((bb, tt, d), lambda i, j: (i, j, 0)),
                   pl.BlockSpec((bb, width - 1, dff), lambda i, j: (i, 0, 0))],
        out_shape=[jax.ShapeDtypeStruct((b, t, d), F32),
                   jax.ShapeDtypeStruct((b, width - 1, dff), F32)],
        scratch_shapes=[pltpu.VMEM((bb, SUBLANES + tt, dff), F32)],
        compiler_params=_cparams(("arbitrary", "arbitrary")),
        name="ffn",
    )(x, g, w_up, st, w_dw, b_dw, w_down)


def _rope_tables(pos):
    half = D_HEAD // 2
    inv = jnp.exp(-math.log(ROPE_BASE) * jnp.arange(half, dtype=F32) / half)
    ang = pos.astype(F32)[:, None] * inv[None, :]
    cos, sin = jnp.cos(ang), jnp.sin(ang)
    cos_h = jnp.concatenate([cos, cos], axis=1)
    sin_h = jnp.concatenate([-sin, sin], axis=1)
    reps = LANES // D_HEAD
    return jnp.tile(cos_h, (1, reps)), jnp.tile(sin_h, (1, reps))


def _layer(x, pos, conv_state, ret_state, fox_past, ffn_state, lw, cfg, layer, depth, stacked):
    (g_mix, w_in, b_fgate, w_dw_conv, b_dw_conv, ln_conv_g, ln_conv_b, w_conv_out,
     gn_ret_g, w_ret_out, g_q_fox, g_k_fox, w_fox_out, w_out,
     g_ffn, w_ffn_up, w_ffn_dw, b_ffn_dw, w_ffn_down) = lw
    b, t, d = x.shape
    n_main = 9 * SEG
    row = lambda v: v.reshape(1, -1).astype(F32)

    w_main = w_in[:, :n_main].astype(BF16)
    w_ffT = jnp.transpose(w_in[:, n_main:n_main + N_HEADS]).astype(BF16)
    w_gates = w_in[:, n_main + N_HEADS:].astype(BF16)
    b_ff = b_fgate.reshape(N_HEADS, 1).astype(F32)
    cos_t, sin_t = _rope_tables(pos)
    head_id = np.arange(SEG) // D_HEAD
    bd = jnp.asarray((head_id[:, None] == head_id[None, :]) / D_HEAD, BF16)
    tile_head = lambda v: jnp.tile(v.astype(F32), N_HEADS).reshape(1, SEG)
    tile_head_t = lambda v: jnp.broadcast_to(jnp.tile(v.astype(F32), N_HEADS)[:, None], (SEG, LANES))

    if fox_past is None:
        wT = jnp.transpose(w_in[:, 6 * SEG:9 * SEG]).astype(BF16)
        (u, rq, rk, rv, sg, fqT, fkb, fvTb, fkT_all, fvT_all, lfT) = _inproj_seq(
            x, row(g_mix), w_main, w_ffT, b_ff, cos_t, sin_t, tile_head(g_k_fox), bd,
            wT, tile_head_t(g_q_fox), tile_head_t(g_k_fox), stacked, layer, depth, tt=cfg["tt"])
        fox_state = (fkT_all, fvT_all, lfT)
        caug = _cumaug(jnp.transpose(lfT, (0, 2, 1)))
        c_br = _fox_prompt(fqT, fkb, fvTb, caug, tq=cfg["tq"])
    else:
        (u, rq, rk, rv, sg, fq, fk, fkb, fv, fvb), lfT = _inproj_rows(
            x, row(g_mix), w_main, w_ffT, b_ff, cos_t, sin_t,
            tile_head(g_q_fox), tile_head(g_k_fox), bd, bb=cfg["bb"], tt=cfg["tt"])
        fox_state = (fk.reshape(b, t, N_HEADS, D_HEAD), fv.reshape(b, t, N_HEADS, D_HEAD),
                     jnp.transpose(lfT, (0, 2, 1)))
        k_past, v_past, logf_past = fox_past
        p_len = k_past.shape[1]
        time_last = lambda a: jnp.transpose(a, (0, 2, 3, 1)).reshape(b, SEG, p_len)
        c_br = _fox_cache(fq, fkb, fvb, time_last(k_past), time_last(v_past),
                          jnp.transpose(logf_past.astype(F32), (0, 2, 1)), lfT)

    a_br = _conv_branch(u, conv_state, w_dw_conv.astype(F32), row(b_dw_conv), row(ln_conv_g), row(ln_conv_b),
                        tt=cfg["tt_seq"])
    new_conv = u[:, t - (w_dw_conv.shape[0] - 1):, :]

    b_br, new_ret = _retention(rq, rk, rv, sg, ret_state.astype(F32), row(gn_ret_g), bd, rows=cfg["ret_rows"])

    n = b * t
    x1 = _merge(x.reshape(n, d), row(g_mix), w_gates, a_br.reshape(n, SEG), b_br.reshape(n, SEG),
                c_br.reshape(n, SEG), w_conv_out.astype(BF16), w_ret_out.astype(BF16),
                w_fox_out.astype(BF16), w_out.astype(BF16), tm=cfg["tm"]).reshape(b, t, d)

    x2, new_ffn = _ffn(x1, row(g_ffn), w_ffn_up.astype(BF16), ffn_state, w_ffn_dw.astype(F32), row(b_ffn_dw),
                       w_ffn_down.astype(BF16), bb=cfg["bb"], tt=cfg["tt_seq"], n_chunks=cfg["ffn_chunks"])
    return x2, (new_conv, new_ret, new_ffn), fox_state


def _group_cfg(b, t):
    if t >= 512:
        return dict(bb=1, tt=512, tt_seq=512, ret_rows=256, tq=512, tm=512, ffn_chunks=2)
    bb = max(1, min(b, 512 // t))
    return dict(bb=bb, tt=t, tt_seq=t, ret_rows=t, tq=t, tm=min(b * t, 512), ffn_chunks=2)


def kernel(x_prompt, x_sample, state_conv, state_ret, cache_fox_k, cache_fox_v, cache_fox_logf, state_ffn_conv, g_mix, w_in, b_fgate, w_dw_conv, b_dw_conv, ln_conv_g, ln_conv_b, w_conv_out, gn_ret_g, w_ret_out, g_q_fox, g_k_fox, w_fox_out, w_out, g_ffn, w_ffn_up, w_ffn_dw, b_ffn_dw, w_ffn_down):
    bp, sp, _ = x_prompt.shape
    bs, ts, _ = x_sample.shape
    depth = w_in.shape[0]
    p_len = cache_fox_k.shape[2]
    conv_w = w_dw_conv.shape[1]
    ffn_w = w_ffn_dw.shape[1]
    d_conv = w_dw_conv.shape[2]
    d_ff = w_ffn_down.shape[1]
    assert sp >= conv_w - 1 and ts >= conv_w - 1 and conv_w - 1 <= HALO and ffn_w - 1 <= SUBLANES
    assert d_conv == SEG and w_in.shape[2] == 9 * SEG + N_HEADS + 3 * x_prompt.shape[2]
    pos_p = jnp.arange(sp)
    pos_s = p_len + jnp.arange(ts)
    cfg_p = _group_cfg(bp, sp)
    cfg_s = _group_cfg(bs, ts)
    weights = (g_mix, w_in, b_fgate, w_dw_conv, b_dw_conv, ln_conv_g, ln_conv_b, w_conv_out,
               gn_ret_g, w_ret_out, g_q_fox, g_k_fox, w_fox_out, w_out,
               g_ffn, w_ffn_up, w_ffn_dw, b_ffn_dw, w_ffn_down)
    xp, xs = x_prompt, x_sample
    st_p, st_s, fox_s = [], [], []
    stacked = None
    for l in range(depth):
        lw = tuple(w[l] for w in weights)
        xp, sp_l, (fkT_all, fvT_all, lfT) = _layer(
            xp, pos_p,
            jnp.zeros((bp, conv_w - 1, d_conv), F32),
            jnp.zeros((bp, N_HEADS, D_HEAD, D_HEAD), F32),
            None,
            jnp.zeros((bp, ffn_w - 1, d_ff), F32), lw, cfg_p, l, depth, stacked)
        stacked = (fkT_all, fvT_all)
        xs, ss_l, fs_l = _layer(xs, pos_s, state_conv[l], state_ret[l],
                                (cache_fox_k[l], cache_fox_v[l], cache_fox_logf[l]),
                                state_ffn_conv[l], lw, cfg_s, l, depth, None)
        st_p.append(sp_l + (lfT,))
        st_s.append(ss_l)
        fox_s.append(fs_l)
    stack = lambda sts, i: jnp.stack([s[i] for s in sts])
    heads_last = lambda a: jnp.transpose(a.reshape(depth, bp, N_HEADS, D_HEAD, sp), (0, 1, 4, 2, 3))
    return (xp, xs,
            stack(st_p, 0), stack(st_p, 1), heads_last(stacked[0]), heads_last(stacked[1]),
            jnp.transpose(stack(st_p, 3), (0, 1, 3, 2)), stack(st_p, 2),
            stack(st_s, 0), stack(st_s, 1), stack(fox_s, 0), stack(fox_s, 1), stack(fox_s, 2), stack(st_s, 2))
```

```python
import functools
import math

import jax
import jax.numpy as jnp
import numpy as np
from jax import lax
from jax.experimental import pallas as pl
from jax.experimental.pallas import tpu as pltpu

F32 = jnp.float32
BF16 = jnp.bfloat16

EPS = 1e-6
ROPE_BASE = 10000.0
CHUNK = 64
N_HEADS = 8
D_HEAD = 64
SEG = N_HEADS * D_HEAD
LANES = 128
SUBLANES = 8
HALO = 32
VMEM_LIMIT = 56 * 1024 * 1024
LOG2E = math.log2(math.e)


def _cparams(sem):
    return pltpu.CompilerParams(dimension_semantics=sem, vmem_limit_bytes=VMEM_LIMIT)


def _dot(a, b):
    return jnp.dot(a, b, preferred_element_type=F32)


def _dot_nt(a, b):
    return lax.dot_general(a, b, (((1,), (1,)), ((), ())), preferred_element_type=F32)


def _dot_tn(a, b):
    return lax.dot_general(a, b, (((0,), (0,)), ((), ())), preferred_element_type=F32)


def _split3(x):
    hi = x.astype(BF16)
    r1 = x - hi.astype(F32)
    mid = r1.astype(BF16)
    lo = (r1 - mid.astype(F32)).astype(BF16)
    return hi, mid, lo


def _dot_exact_rhs(a_f32, b_bf16):
    hi, mid, lo = _split3(a_f32)
    return _dot(hi, b_bf16) + _dot(mid, b_bf16) + _dot(lo, b_bf16)


def _dot_exact_lhs(a_bf16, b_f32):
    hi, mid, lo = _split3(b_f32)
    return _dot(a_bf16, hi) + _dot(a_bf16, mid) + _dot(a_bf16, lo)


def _sigmoid(x):
    return 1.0 / (1.0 + jnp.exp(-x))


def _silu(x):
    return x * _sigmoid(x)


def _rms_rows(x, g):
    return x * lax.rsqrt(jnp.mean(x * x, axis=-1, keepdims=True) + EPS) * g


def _inproj_common(x_ref, g_ref, w_ref, cos_ref, sin_ref, u_ref, rq_ref, rk_ref, rv_ref, sg_ref):
    bb, tt, d = x_ref.shape
    tm = bb * tt
    x = x_ref[...].reshape(tm, d)
    h = _rms_rows(x, g_ref[...]).astype(BF16)

    def proj(i):
        return _dot(h, w_ref[:, i * SEG:(i + 1) * SEG])

    def put(ref, val):
        ref[...] = val.reshape(bb, tt, SEG).astype(ref.dtype)

    put(u_ref, proj(0) * _sigmoid(proj(1)))

    cos = cos_ref[...][None]
    sin = sin_ref[...][None]
    half = D_HEAD // 2
    lane = lax.broadcasted_iota(jnp.int32, (1, LANES), 1)
    first_half = (lane % D_HEAD) < half

    def rotary(p, scale):
        cols = []
        for c in range(SEG // LANES):
            pc = p[:, c * LANES:(c + 1) * LANES]
            partner = jnp.where(first_half, pltpu.roll(pc, LANES - half, 1), pltpu.roll(pc, half, 1))
            r = pc.reshape(bb, tt, LANES) * cos + partner.reshape(bb, tt, LANES) * sin
            cols.append(r * scale)
        return jnp.concatenate(cols, axis=-1)

    rq_ref[...] = rotary(proj(2), 1.0).astype(BF16)
    rk_ref[...] = rotary(proj(3), D_HEAD ** -0.5).astype(BF16)
    put(rv_ref, proj(4))
    put(sg_ref, _silu(proj(5)))
    return h, proj, put


def _head_rms(p, g, bd):
    ms = _dot((p * p).astype(BF16), bd)
    return p * lax.rsqrt(ms + EPS) * g


def _head_rms_t(z, gT_ref):
    tm = z.shape[1]
    z3 = z.reshape(N_HEADS, D_HEAD, tm)
    ms = jnp.mean(z3 * z3, axis=1, keepdims=True)
    g = jnp.concatenate([gT_ref[...]] * (tm // LANES), axis=1).reshape(N_HEADS, D_HEAD, tm)
    return (z3 * lax.rsqrt(ms + EPS) * g).reshape(SEG, tm)


def _log_forget(wff_ref, bff_ref, h):
    z = _dot_nt(wff_ref[...], h) + bff_ref[...]
    return jnp.minimum(z, 0.0) - jnp.log(1.0 + jnp.exp(-jnp.abs(z)))


def _inproj_seq_kernel(x_ref, g_ref, w_ref, wff_ref, bff_ref, cos_ref, sin_ref,
                       wT_ref, gqT_ref, gkT_ref, *rest):
    (u_ref, rq_ref, rk_ref, rv_ref, sg_ref, fqT_ref, fkb_ref, fvTb_ref, fkT_ref, fvT_ref, lf_ref) = rest[-11:]
    h, proj, put = _inproj_common(x_ref, g_ref, w_ref, cos_ref, sin_ref, u_ref, rq_ref, rk_ref, rv_ref, sg_ref)
    zq = _dot_nt(wT_ref[0:SEG, :], h)
    fqT_ref[0] = (_head_rms_t(zq, gqT_ref) * (D_HEAD ** -0.5 * LOG2E)).astype(BF16)
    zk = _dot_nt(wT_ref[SEG:2 * SEG, :], h)
    fkT = _head_rms_t(zk, gkT_ref)
    fkT_ref[0, 0] = fkT
    put(fkb_ref, jnp.transpose(fkT))
    fvT = _dot_nt(wT_ref[2 * SEG:3 * SEG, :], h)
    fvT_ref[0, 0] = fvT
    fvTb_ref[0] = fvT.astype(BF16)
    lf_ref[0] = _log_forget(wff_ref, bff_ref, h)


def _inproj_rows_kernel(x_ref, g_ref, w_ref, wff_ref, bff_ref, cos_ref, sin_ref, gq_ref, gk_ref, bd_ref,
                        u_ref, rq_ref, rk_ref, rv_ref, sg_ref, fq_ref, fk_ref, fkb_ref, fv_ref, fvb_ref, lf_ref):
    h, proj, put = _inproj_common(x_ref, g_ref, w_ref, cos_ref, sin_ref, u_ref, rq_ref, rk_ref, rv_ref, sg_ref)
    bd = bd_ref[...]
    put(fq_ref, _head_rms(proj(6), gq_ref[...], bd) * (D_HEAD ** -0.5))
    fk = _head_rms(proj(7), gk_ref[...], bd)
    put(fk_ref, fk)
    put(fkb_ref, fk)
    fv = proj(8)
    put(fv_ref, fv)
    put(fvb_ref, fv)
    lf_ref[0] = _log_forget(wff_ref, bff_ref, h)


def _inproj_seq(x, g, w_main, w_ffT, b_ff, cos_t, sin_t, wT, gqT, gkT, stacked, layer, depth, *, tt):
    b, t, d = x.shape
    act = lambda: pl.BlockSpec((1, tt, SEG), lambda i, j: (j, i, 0))
    act_t = lambda: pl.BlockSpec((1, SEG, tt), lambda i, j: (j, 0, i))
    stk = lambda: pl.BlockSpec((1, 1, SEG, tt), lambda i, j: (layer, j, 0, i))
    const = lambda shape: pl.BlockSpec(shape, lambda i, j: (0,) * len(shape))
    nat = lambda dt: jax.ShapeDtypeStruct((b, t, SEG), dt)
    tr = jax.ShapeDtypeStruct((b, SEG, t), BF16)
    stacked_shape = jax.ShapeDtypeStruct((depth, b, SEG, t), F32)
    out_shape = [nat(F32), nat(BF16), nat(BF16), nat(BF16), nat(BF16),
                 tr, nat(BF16), tr,
                 stacked_shape, stacked_shape,
                 jax.ShapeDtypeStruct((b, N_HEADS, t), F32)]
    out_specs = [act(), act(), act(), act(), act(), act_t(), act(), act_t(), stk(), stk(),
                 pl.BlockSpec((1, N_HEADS, tt), lambda i, j: (j, 0, i))]
    operands = [x, g, w_main, w_ffT, b_ff, cos_t, sin_t, wT, gqT, gkT]
    in_specs = [pl.BlockSpec((1, tt, d), lambda i, j: (j, i, 0)),
                const((1, d)), const(w_main.shape), const(w_ffT.shape), const((N_HEADS, 1)),
                pl.BlockSpec((tt, LANES), lambda i, j: (i, 0)), pl.BlockSpec((tt, LANES), lambda i, j: (i, 0)),
                const(wT.shape), const(gqT.shape), const(gkT.shape)]
    aliases = {}
    if stacked is not None:
        aliases = {len(operands): 8, len(operands) + 1: 9}
        operands += list(stacked)
        in_specs += [pl.BlockSpec(memory_space=pl.ANY)] * 2
    return pl.pallas_call(
        _inproj_seq_kernel,
        grid=(t // tt, b),
        in_specs=in_specs,
        out_specs=out_specs,
        out_shape=out_shape,
        input_output_aliases=aliases,
        compiler_params=_cparams(("arbitrary", "arbitrary")),
        name="inproj_seq",
    )(*operands)


def _inproj_rows(x, g, w_main, w_ffT, b_ff, cos_t, sin_t, gq, gk, bd, *, bb, tt):
    b, t, d = x.shape
    n_t, n_b = t // tt, b // bb
    tm = bb * tt
    act = lambda: pl.BlockSpec((bb, tt, SEG), lambda i, j: (j, i, 0))
    const = lambda shape: pl.BlockSpec(shape, lambda i, j: (0,) * len(shape))
    nat = lambda dt: jax.ShapeDtypeStruct((b, t, SEG), dt)
    out_shape = [nat(F32), nat(BF16), nat(BF16), nat(BF16), nat(BF16), nat(BF16),
                 nat(F32), nat(BF16), nat(F32), nat(BF16),
                 jax.ShapeDtypeStruct((n_t * n_b, N_HEADS, tm), F32)]
    out_specs = [act() for _ in range(10)] + [pl.BlockSpec((1, N_HEADS, tm), lambda i, j: (i * n_b + j, 0, 0))]
    outs = pl.pallas_call(
        _inproj_rows_kernel,
        grid=(n_t, n_b),
        in_specs=[pl.BlockSpec((bb, tt, d), lambda i, j: (j, i, 0)),
                  const((1, d)), const(w_main.shape), const(w_ffT.shape), const((N_HEADS, 1)),
                  pl.BlockSpec((tt, LANES), lambda i, j: (i, 0)), pl.BlockSpec((tt, LANES), lambda i, j: (i, 0)),
                  const((1, SEG)), const((1, SEG)), const((SEG, SEG))],
        out_specs=out_specs,
        out_shape=out_shape,
        compiler_params=_cparams(("arbitrary", "arbitrary")),
        name="inproj_rows",
    )(x, g, w_main, w_ffT, b_ff, cos_t, sin_t, gq, gk, bd)
    lf = outs[-1].reshape(n_t, n_b, N_HEADS, bb, tt)
    lf = jnp.transpose(lf, (1, 3, 2, 0, 4)).reshape(b, N_HEADS, t)
    return outs[:-1], lf


CONV_ROWS = 32


def _conv_context(st_ref, xbuf, tt, first_tile):
    @pl.when(first_tile)
    def _():
        xbuf[:, 0:HALO, :] = st_ref[...]

    @pl.when(jnp.logical_not(first_tile))
    def _():
        xbuf[:, 0:HALO, :] = xbuf[:, tt:tt + HALO, :]


def _conv_pieces(u_ref, w_ref, b_ref, lg_ref, lb_ref, a_out, xbuf, xsh):
    bb, tt, c = u_ref.shape
    width = w_ref.shape[0]
    rows = min(tt, CONV_ROWS)
    n_rows = HALO + tt
    first = HALO - (width - 1)

    def fill():
        xbuf[:, HALO:n_rows, :] = u_ref[...]
        xbuf[:, n_rows:n_rows + SUBLANES, :] = jnp.zeros((bb, SUBLANES, c), F32)

    def shift(s, s0):
        win = xbuf[s, s0:s0 + rows + SUBLANES, :]
        for sh in range(1, SUBLANES):
            xsh[sh - 1, s, s0:s0 + rows, :] = win[sh:sh + rows, :]

    def taps(s, r0):
        acc = jnp.zeros((rows, c), F32) + b_ref[...]
        for j in range(width):
            whole, sh = divmod(first + j, SUBLANES)
            lo = r0 + whole * SUBLANES
            tap = xbuf[s, lo:lo + rows, :] if sh == 0 else xsh[sh - 1, s, lo:lo + rows, :]
            acc = acc + w_ref[j:j + 1, :] * tap
        mu = jnp.mean(acc, axis=-1, keepdims=True)
        xc = acc - mu
        var = jnp.mean(xc * xc, axis=-1, keepdims=True)
        y = xc * lax.rsqrt(var + EPS) * lg_ref[...] + lb_ref[...]
        a_out[s * tt + r0:s * tt + r0 + rows, :] = _silu(y).astype(a_out.dtype)

    pieces = [fill]
    for s in range(bb):
        pieces += [functools.partial(shift, s, s0) for s0 in range(0, n_rows, rows)]
        pieces += [functools.partial(taps, s, r0) for r0 in range(0, tt, rows)]
    return pieces


def _interleave(major, minor):
    done = 0
    for i, piece in enumerate(major):
        piece()
        upto = (i + 1) * len(minor) // len(major)
        for filler in minor[done:upto]:
            filler()
        done = upto


def _ret_kernel(q_ref, k_ref, v_ref, sg_ref, st0_ref, dmat_ref, qdec_ref, kdec_ref, gn_ref, bd_ref,
                o_ref, st_ref, oall, *, tile_decay):
    t = pl.program_id(1)

    @pl.when(t == 0)
    def _():
        st_ref[...] = st0_ref[...]

    q = q_ref[0]
    k = k_ref[0]
    q_in = (q.astype(F32) * qdec_ref[...]).astype(BF16)
    k_end = (k.astype(F32) * kdec_ref[...]).astype(BF16)
    for h in range(N_HEADS):
        sl = slice(h * D_HEAD, (h + 1) * D_HEAD)
        qh, kh, vh = q[:, sl], k[:, sl], v_ref[0, :, sl]
        s = _dot_nt(qh, kh) * dmat_ref[h]
        state = st_ref[0, h]
        o = _dot(s.astype(BF16), vh) + _dot(q_in[:, sl], state.astype(BF16))
        st_ref[0, h] = state * tile_decay[h] + _dot_tn(k_end[:, sl], vh)
        oall[:, sl] = o
    o = oall[...]
    bd = bd_ref[...]
    o_hi = o.astype(BF16)
    o_lo = (o - o_hi.astype(F32)).astype(BF16)
    oc = o - (_dot(o_hi, bd) + _dot(o_lo, bd))
    var = _dot((oc * oc).astype(BF16), bd)
    o_ref[0] = (oc * lax.rsqrt(var + EPS) * gn_ref[...] * sg_ref[0].astype(F32)).astype(o_ref.dtype)


def _ret_tables(rows, chunk):
    lg = np.log1p(-np.exp2(-5.0 - np.arange(N_HEADS, dtype=np.float64)))
    idx = np.arange(rows, dtype=np.float64)
    dist = idx[:, None] - idx[None, :]
    visible = (idx[None, :] // chunk) <= (idx[:, None] // chunk)
    dmat = np.where(visible[None], np.exp(lg[:, None, None] * np.abs(dist)[None]), 0.0)
    qdec = np.repeat(np.exp(lg[None, :] * (idx[:, None] + 1.0)), D_HEAD, axis=1)
    kdec = np.repeat(np.exp(lg[None, :] * (rows - 1.0 - idx[:, None])), D_HEAD, axis=1)
    tile_decay = tuple(float(v) for v in np.exp(lg * rows))
    return (jnp.asarray(dmat, F32), jnp.asarray(qdec, F32), jnp.asarray(kdec, F32), tile_decay)


def _retention(q, k, v, sg, state0, gn_g, bd, *, rows):
    b, t, c = q.shape
    chunk = min(t, CHUNK)
    dmat, qdec, kdec, tile_decay = _ret_tables(rows, chunk)
    const = lambda shape: pl.BlockSpec(shape, lambda i, j: (0,) * len(shape))
    act = lambda: pl.BlockSpec((1, rows, c), lambda i, j: (i, j, 0))
    st_spec = lambda: pl.BlockSpec((1, N_HEADS, D_HEAD, D_HEAD), lambda i, j: (i, 0, 0, 0))
    return pl.pallas_call(
        functools.partial(_ret_kernel, tile_decay=tile_decay),
        grid=(b, t // rows),
        in_specs=[act(), act(), act(), act(), st_spec(),
                  const(dmat.shape), const(qdec.shape), const(kdec.shape), const((1, c)), const(bd.shape)],
        out_specs=[act(), st_spec()],
        out_shape=[jax.ShapeDtypeStruct((b, t, c), BF16),
                   jax.ShapeDtypeStruct((b, N_HEADS, D_HEAD, D_HEAD), F32)],
        scratch_shapes=[pltpu.VMEM((rows, c), F32)],
        compiler_params=_cparams(("arbitrary", "arbitrary")),
        name="ret",
    )(q, k, v, sg, state0, dmat, qdec, kdec, gn_g, bd)


CUM_BLOCK = 256
AUG_PIECES = 3


def _cumaug_kernel(lf_ref, o_ref):
    t = lf_ref.shape[1]
    blk = min(CUM_BLOCK, t)
    r = lax.broadcasted_iota(jnp.int32, (blk, blk), 0)
    c = lax.broadcasted_iota(jnp.int32, (blk, blk), 1)
    lower = (c <= r).astype(BF16)
    sh = lax.broadcasted_iota(jnp.int32, (N_HEADS, LANES), 0)
    sl = lax.broadcasted_iota(jnp.int32, (N_HEADS, LANES), 1)
    place = [(sl == AUG_PIECES * sh + p).astype(BF16) for p in range(AUG_PIECES)]
    carry = jnp.zeros((1, N_HEADS), F32)
    for i in range(t // blk):
        cs = _dot_exact_lhs(lower, lf_ref[0, i * blk:(i + 1) * blk, :]) + carry
        carry = cs[blk - 1:blk, :]
        pieces = _split3(-LOG2E * cs)
        aug = sum(_dot(piece, sel) for piece, sel in zip(pieces, place))
        o_ref[0, i * blk:(i + 1) * blk, :] = aug.astype(o_ref.dtype)


def _cumaug(logf):
    b, t, h = logf.shape
    return pl.pallas_call(
        _cumaug_kernel,
        grid=(b,),
        in_specs=[pl.BlockSpec((1, t, h), lambda i: (i, 0, 0))],
        out_specs=pl.BlockSpec((1, t, LANES), lambda i: (i, 0, 0)),
        out_shape=jax.ShapeDtypeStruct((b, t, LANES), BF16),
        compiler_params=_cparams(("arbitrary",)),
        name="cumaug",
    )(logf)


ONES_ROWS = 16


def _fox_kernel(qi_ref, kj_ref, qT_ref, k_ref, ca_ref, vT_ref, o_ref, qaug, m_ref, acc_ref, s0_ref, s1_ref):
    s = pl.program_id(1)
    i = qi_ref[s]
    j = kj_ref[s]
    tq = qT_ref.shape[2]
    tk = k_ref.shape[1]
    s_bufs = (s0_ref, s1_ref)

    @pl.when(j == 0)
    def _():
        row = lax.broadcasted_iota(jnp.int32, (LANES, tq), 0)
        for h in range(N_HEADS):
            qT = qT_ref[0, (h // 2) * LANES:(h // 2 + 1) * LANES, :]
            qaug[h, 0:LANES, :] = jnp.where(row // D_HEAD == h % 2, qT, jnp.zeros_like(qT))
            pick = (row >= AUG_PIECES * h) & (row < AUG_PIECES * (h + 1))
            qaug[h, LANES:2 * LANES, :] = jnp.where(pick, 1.0, 0.0).astype(BF16)
        m_ref[...] = jnp.full(m_ref.shape, -jnp.inf, F32)
        acc_ref[...] = jnp.zeros(acc_ref.shape, F32)

    def step(diagonal):
        ca = ca_ref[0]
        ones = jnp.ones((ONES_ROWS, tk), BF16)

        def logits(h):
            kk = jnp.concatenate([k_ref[0, :, (h // 2) * LANES:(h // 2 + 1) * LANES], ca], axis=1)
            s_bufs[h % 2][...] = _dot(kk, qaug[h])

        logits(0)
        for h in range(N_HEADS):
            if h + 1 < N_HEADS:
                logits(h + 1)
            sT = s_bufs[h % 2][...]
            if diagonal:
                key = lax.broadcasted_iota(jnp.int32, (tk, tq), 0)
                qry = lax.broadcasted_iota(jnp.int32, (tk, tq), 1)
                sT = jnp.where(key <= qry, sT, -jnp.inf)
            m_old = m_ref[h]
            m_new = jnp.maximum(m_old, jnp.max(sT, axis=0, keepdims=True))
            alpha = jnp.exp2(m_old - m_new)
            p = jnp.exp2(sT - m_new).astype(BF16)
            lhs = jnp.concatenate([vT_ref[0, h * D_HEAD:(h + 1) * D_HEAD, :], ones], axis=0)
            acc_ref[h] = alpha * acc_ref[h] + _dot(lhs, p)
            m_ref[h] = m_new

    @pl.when(j < i)
    def _():
        step(False)

    @pl.when(j == i)
    def _():
        step(True)
        for hp in range(N_HEADS // 2):
            outs = []
            for h in (2 * hp, 2 * hp + 1):
                acc = acc_ref[h]
                outs.append(acc[0:D_HEAD, :] / acc[D_HEAD:D_HEAD + 1, :])
            o_ref[0, :, hp * LANES:(hp + 1) * LANES] = (
                jnp.transpose(jnp.concatenate(outs, axis=0)).astype(o_ref.dtype))


def _fox_prompt(qT, k, vT, caug, *, tq):
    b, c, t = qT.shape
    n_q = t // tq
    pairs = [(i, j) for i in range(n_q) for j in range(i + 1)]
    qi = jnp.asarray([p[0] for p in pairs], jnp.int32)
    kj = jnp.asarray([p[1] for p in pairs], jnp.int32)
    grid_spec = pltpu.PrefetchScalarGridSpec(
        num_scalar_prefetch=2,
        grid=(b, len(pairs)),
        in_specs=[pl.BlockSpec((1, c, tq), lambda bi, s, qi, kj: (bi, 0, qi[s])),
                  pl.BlockSpec((1, tq, c), lambda bi, s, qi, kj: (bi, kj[s], 0)),
                  pl.BlockSpec((1, tq, LANES), lambda bi, s, qi, kj: (bi, kj[s], 0)),
                  pl.BlockSpec((1, c, tq), lambda bi, s, qi, kj: (bi, 0, kj[s]))],
        out_specs=pl.BlockSpec((1, tq, c), lambda bi, s, qi, kj: (bi, qi[s], 0)),
        scratch_shapes=[pltpu.VMEM((N_HEADS, 2 * LANES, tq), BF16), pltpu.VMEM((N_HEADS, 1, tq), F32),
                        pltpu.VMEM((N_HEADS, D_HEAD + ONES_ROWS, tq), F32),
                        pltpu.VMEM((tq, tq), F32), pltpu.VMEM((tq, tq), F32)])
    return pl.pallas_call(
        _fox_kernel,
        grid_spec=grid_spec,
        out_shape=jax.ShapeDtypeStruct((b, t, c), BF16),
        compiler_params=_cparams(("arbitrary", "arbitrary")),
        name="fox",
    )(qi, kj, qT, k, caug, vT)


def _fox_cache_kernel(q_ref, kn_ref, vn_ref, kpT_ref, vpT_ref, lfp_ref, lfn_ref, o_ref, cum_ref):
    t = q_ref.shape[1]
    p_len = kpT_ref.shape[3]
    c = q_ref.shape[2]
    n_rows = N_HEADS * t

    r = lax.broadcasted_iota(jnp.int32, (LANES, LANES), 0)
    cc = lax.broadcasted_iota(jnp.int32, (LANES, LANES), 1)
    upper = (r <= cc).astype(BF16)
    carry = jnp.zeros((N_HEADS, 1), F32)
    for i in range(p_len // LANES):
        cs = _dot_exact_rhs(lfp_ref[0, 0, :, i * LANES:(i + 1) * LANES], upper) + carry
        cum_ref[:, i * LANES:(i + 1) * LANES] = cs
        carry = cs[:, LANES - 1:LANES]
    cum_new = _dot_exact_rhs(lfn_ref[0], upper[:t, :t]) + carry

    def per_query_rows(cum):
        n = cum.shape[1]
        return jnp.broadcast_to(cum[:, None, :], (N_HEADS, t, n)).reshape(n_rows, n)

    q = q_ref[0]
    qrep = jnp.concatenate([q] * N_HEADS, axis=0)
    qr = lax.broadcasted_iota(jnp.int32, (n_rows, c), 0)
    ql = lax.broadcasted_iota(jnp.int32, (n_rows, c), 1)
    qbd = jnp.where(qr // t == ql // D_HEAD, qrep, jnp.zeros_like(qrep))

    s_past = _dot(qbd, kpT_ref[0, 0].astype(BF16)) - per_query_rows(cum_ref[...])
    s_new = _dot_nt(qbd, kn_ref[0]) - per_query_rows(cum_new)
    qry = lax.broadcasted_iota(jnp.int32, (n_rows, t), 0) % t
    key = lax.broadcasted_iota(jnp.int32, (n_rows, t), 1)
    s_new = jnp.where(key <= qry, s_new, -jnp.inf)

    m = jnp.maximum(jnp.max(s_past, axis=1, keepdims=True), jnp.max(s_new, axis=1, keepdims=True))
    p_past = jnp.exp(s_past - m)
    p_new = jnp.exp(s_new - m)
    inv = 1.0 / (jnp.sum(p_past, axis=1, keepdims=True) + jnp.sum(p_new, axis=1, keepdims=True))
    o_full = (_dot_nt((p_past * inv).astype(BF16), vpT_ref[0, 0].astype(BF16))
              + _dot((p_new * inv).astype(BF16), vn_ref[0]))
    for h in range(N_HEADS):
        o_ref[0, :, h * D_HEAD:(h + 1) * D_HEAD] = (
            o_full[h * t:(h + 1) * t, h * D_HEAD:(h + 1) * D_HEAD].astype(o_ref.dtype))


def _fox_cache(q, kn, vn, kpT, vpT, lf_pastT, lf_newT, layer):
    b, t, c = q.shape
    p_len = kpT.shape[3]
    assert p_len % LANES == 0 and t <= LANES
    new = lambda: pl.BlockSpec((1, t, c), lambda i: (i, 0, 0))
    past = lambda: pl.BlockSpec((1, 1, c, p_len), lambda i: (layer, i, 0, 0))
    return pl.pallas_call(
        _fox_cache_kernel,
        grid=(b,),
        in_specs=[new(), new(), new(), past(), past(),
                  pl.BlockSpec((1, 1, N_HEADS, p_len), lambda i: (layer, i, 0, 0)),
                  pl.BlockSpec((1, N_HEADS, t), lambda i: (i, 0, 0))],
        out_specs=new(),
        out_shape=jax.ShapeDtypeStruct((b, t, c), BF16),
        scratch_shapes=[pltpu.VMEM((N_HEADS, p_len), F32)],
        compiler_params=_cparams(("arbitrary",)),
        name="fox_cache",
    )(q, kn, vn, kpT, vpT, lf_pastT, lf_newT)


def _merge_kernel(x_ref, g_ref, wg_ref, u_ref, st_ref, wdw_ref, bdw_ref, lg_ref, lb_ref, b_ref, c_ref,
                  wa_ref, wb_ref, wc_ref, wo_ref, o_ref, xbuf, xsh, a_br):
    bb, tt, d = x_ref.shape
    tm = bb * tt
    _conv_context(st_ref, xbuf, tt, pl.program_id(1) == 0)
    x = x_ref[...].reshape(tm, d)
    h = _rms_rows(x, g_ref[...]).astype(BF16)

    block = 2 * LANES
    n_blocks = d // block
    gate_a = [None] * n_blocks
    merged = [None] * n_blocks

    def gate(n, i):
        return _sigmoid(_dot(h, wg_ref[:, n * d + i * block:n * d + (i + 1) * block]))

    def piece_a(i):
        gate_a[i] = gate(0, i)

    def piece_b(i):
        merged[i] = gate(1, i) * _dot(b_ref[...].reshape(tm, SEG), wb_ref[:, i * block:(i + 1) * block])

    def piece_c(i):
        merged[i] = merged[i] + gate(2, i) * _dot(c_ref[...].reshape(tm, SEG), wc_ref[:, i * block:(i + 1) * block])

    matmuls = [functools.partial(p, i) for i in range(n_blocks) for p in (piece_a, piece_b, piece_c)]
    _interleave(matmuls, _conv_pieces(u_ref, wdw_ref, bdw_ref, lg_ref, lb_ref, a_br, xbuf, xsh))
    merged = jnp.concatenate(merged, axis=1) + jnp.concatenate(gate_a, axis=1) * _dot(a_br[...], wa_ref[...])
    o_ref[...] = (x + _dot(merged.astype(BF16), wo_ref[...])).reshape(bb, tt, d)


def _merge(x, g, w_gates, u, conv_state, w_dw, b_dw, ln_g, ln_b, bm, c, wa, wb, wc, wo, *, bb, tt):
    b, t, d = x.shape
    width = w_dw.shape[0]
    st = jnp.pad(conv_state.astype(F32), ((0, 0), (HALO - (width - 1), 0), (0, 0)))
    const = lambda shape: pl.BlockSpec(shape, lambda i, j: (0,) * len(shape), pipeline_mode=pl.Buffered(1))
    act = lambda width: pl.BlockSpec((bb, tt, width), lambda i, j: (i, j, 0))
    return pl.pallas_call(
        _merge_kernel,
        grid=(b // bb, t // tt),
        in_specs=[act(d), const((1, d)), const(w_gates.shape), act(SEG),
                  pl.BlockSpec((bb, HALO, SEG), lambda i, j: (i, 0, 0)),
                  const(w_dw.shape), const((1, SEG)), const((1, SEG)), const((1, SEG)),
                  act(SEG), act(SEG),
                  const(wa.shape), const(wb.shape), const(wc.shape), const(wo.shape)],
        out_specs=act(d),
        out_shape=jax.ShapeDtypeStruct((b, t, d), F32),
        scratch_shapes=[pltpu.VMEM((bb, HALO + tt + SUBLANES, SEG), F32),
                        pltpu.VMEM((SUBLANES - 1, bb, HALO + tt, SEG), F32),
                        pltpu.VMEM((bb * tt, SEG), BF16)],
        compiler_params=_cparams(("arbitrary", "arbitrary")),
        name="merge",
    )(x, g, w_gates, u, st, w_dw, b_dw, ln_g, ln_b, bm, c, wa, wb, wc, wo)


def _erf(x):
    return lax.erf(x)


def _ffn_kernel(x_ref, g_ref, wup_ref, st_ref, wdw_ref, bdw_ref, wdn_ref, o_ref, buf_ref, abuf, *, n_chunks):
    bb, tt, d = x_ref.shape
    tm = bb * tt
    dff = wdn_ref.shape[0]
    cf = dff // n_chunks
    width = wdw_ref.shape[0]
    t = pl.program_id(1)
    x = x_ref[...].reshape(tm, d)
    h = _rms_rows(x, g_ref[...]).astype(BF16)

    @pl.when(t == 0)
    def _():
        abuf[:, 0:SUBLANES, :] = st_ref[...]

    @pl.when(t > 0)
    def _():
        abuf[:, 0:SUBLANES, :] = abuf[:, tt:tt + SUBLANES, :]

    out = x
    first = SUBLANES - (width - 1)
    for n in range(n_chunks):
        cs = slice(n * cf, (n + 1) * cf)
        abuf[:, SUBLANES:SUBLANES + tt, cs] = _dot(h, wup_ref[:, cs]).reshape(bb, tt, cf)
        gate = _dot(h, wup_ref[:, dff + n * cf:dff + (n + 1) * cf])
        conv = jnp.zeros((bb, tt, cf), F32) + bdw_ref[:, cs]
        for j in range(width):
            conv = conv + wdw_ref[j:j + 1, cs] * abuf[:, first + j:first + j + tt, cs]
        conv = conv.reshape(tm, cf)
        act = 0.5 * conv * (1.0 + _erf(conv * (2.0 ** -0.5))) * gate
        out = out + _dot(act.astype(BF16), wdn_ref[cs, :])
    o_ref[...] = out.reshape(bb, tt, d)
    buf_ref[...] = abuf[:, SUBLANES + tt - (width - 1):SUBLANES + tt, :]


def _ffn(x, g, w_up, state, w_dw, b_dw, w_down, *, bb, tt, n_chunks):
    b, t, d = x.shape
    dff = w_down.shape[0]
    width = w_dw.shape[0]
    st = jnp.pad(state.astype(F32), ((0, 0), (SUBLANES - (width - 1), 0), (0, 0)))
    const = lambda shape: pl.BlockSpec(shape, lambda i, j: (0,) * len(shape), pipeline_mode=pl.Buffered(1))
    return pl.pallas_call(
        functools.partial(_ffn_kernel, n_chunks=n_chunks),
        grid=(b // bb, t // tt),
        in_specs=[pl.BlockSpec((bb, tt, d), lambda i, j: (i, j, 0)),
                  const((1, d)), const(w_up.shape),
                  pl.BlockSpec((bb, SUBLANES, dff), lambda i, j: (i, 0, 0)),
                  const(w_dw.shape), const((1, dff)), const(w_down.shape)],
        out_specs=[pl.BlockSpec((bb, tt, d), lambda i, j: (i, j, 0)),
                   pl.BlockSpec((bb, width - 1, dff), lambda i, j: (i, 0, 0))],
        out_shape=[jax.ShapeDtypeStruct((b, t, d), F32),
                   jax.ShapeDtypeStruct((b, width - 1, dff), F32)],
        scratch_shapes=[pltpu.VMEM((bb, SUBLANES + tt, dff), F32)],
        compiler_params=_cparams(("arbitrary", "arbitrary")),
        name="ffn",
    )(x, g, w_up, st, w_dw, b_dw, w_down)


def _rope_tables(pos):
    half = D_HEAD // 2
    inv = jnp.exp(-math.log(ROPE_BASE) * jnp.arange(half, dtype=F32) / half)
    ang = pos.astype(F32)[:, None] * inv[None, :]
    cos, sin = jnp.cos(ang), jnp.sin(ang)
    cos_h = jnp.concatenate([cos, cos], axis=1)
    sin_h = jnp.concatenate([-sin, sin], axis=1)
    reps = LANES // D_HEAD
    return jnp.tile(cos_h, (1, reps)), jnp.tile(sin_h, (1, reps))


def _layer(x, pos, conv_state, ret_state, fox_past, ffn_state, lw, cfg, layer, depth, stacked):
    (g_mix, w_in, b_fgate, w_dw_conv, b_dw_conv, ln_conv_g, ln_conv_b, w_conv_out,
     gn_ret_g, w_ret_out, g_q_fox, g_k_fox, w_fox_out, w_out,
     g_ffn, w_ffn_up, w_ffn_dw, b_ffn_dw, w_ffn_down) = lw
    b, t, d = x.shape
    n_main = 9 * SEG
    row = lambda v: v.reshape(1, -1).astype(F32)

    w_main = w_in[:, :n_main].astype(BF16)
    w_ffT = jnp.transpose(w_in[:, n_main:n_main + N_HEADS]).astype(BF16)
    w_gates = w_in[:, n_main + N_HEADS:].astype(BF16)
    b_ff = b_fgate.reshape(N_HEADS, 1).astype(F32)
    cos_t, sin_t = _rope_tables(pos)
    head_id = np.arange(SEG) // D_HEAD
    bd = jnp.asarray((head_id[:, None] == head_id[None, :]) / D_HEAD, BF16)
    tile_head = lambda v: jnp.tile(v.astype(F32), N_HEADS).reshape(1, SEG)
    tile_head_t = lambda v: jnp.broadcast_to(jnp.tile(v.astype(F32), N_HEADS)[:, None], (SEG, LANES))

    if fox_past is None:
        wT = jnp.transpose(w_in[:, 6 * SEG:9 * SEG]).astype(BF16)
        (u, rq, rk, rv, sg, fqT, fkb, fvTb, fkT_all, fvT_all, lfT) = _inproj_seq(
            x, row(g_mix), w_main[:, :6 * SEG], w_ffT, b_ff, cos_t, sin_t,
            wT, tile_head_t(g_q_fox), tile_head_t(g_k_fox), stacked, layer, depth, tt=cfg["tt"])
        fox_state = (fkT_all, fvT_all, lfT)
        caug = _cumaug(jnp.transpose(lfT, (0, 2, 1)))
        c_br = _fox_prompt(fqT, fkb, fvTb, caug, tq=cfg["tq"])
    else:
        (u, rq, rk, rv, sg, fq, fk, fkb, fv, fvb), lfT = _inproj_rows(
            x, row(g_mix), w_main, w_ffT, b_ff, cos_t, sin_t,
            tile_head(g_q_fox), tile_head(g_k_fox), bd, bb=cfg["bb"], tt=cfg["tt"])
        fox_state = (fk.reshape(b, t, N_HEADS, D_HEAD), fv.reshape(b, t, N_HEADS, D_HEAD),
                     jnp.transpose(lfT, (0, 2, 1)))
        k_all, v_all, logf_all = fox_past
        p_len = k_all.shape[2]
        time_last = lambda a: jnp.transpose(a, (0, 1, 3, 4, 2)).reshape(depth, b, SEG, p_len)
        c_br = _fox_cache(fq, fkb, fvb, time_last(k_all), time_last(v_all),
                          jnp.transpose(logf_all.astype(F32), (0, 1, 3, 2)), lfT, layer)

    new_conv = u[:, t - (w_dw_conv.shape[0] - 1):, :]

    b_br, new_ret = _retention(rq, rk, rv, sg, ret_state.astype(F32), row(gn_ret_g), bd, rows=cfg["ret_rows"])

    x1 = _merge(x, row(g_mix), w_gates, u, conv_state, w_dw_conv.astype(F32), row(b_dw_conv),
                row(ln_conv_g), row(ln_conv_b), b_br, c_br, w_conv_out.astype(BF16), w_ret_out.astype(BF16),
                w_fox_out.astype(BF16), w_out.astype(BF16), bb=cfg["bb"], tt=cfg["tt_seq"])

    x2, new_ffn = _ffn(x1, row(g_ffn), w_ffn_up.astype(BF16), ffn_state, w_ffn_dw.astype(F32), row(b_ffn_dw),
                       w_ffn_down.astype(BF16), bb=cfg["bb"], tt=cfg["tt_seq"], n_chunks=cfg["ffn_chunks"])
    return x2, (new_conv, new_ret, new_ffn), fox_state


def _group_cfg(b, t):
    if t >= 512:
        return dict(bb=1, tt=512, tt_seq=512, ret_rows=256, tq=512, tm=512, ffn_chunks=2)
    bb = max(1, min(b, 512 // t))
    return dict(bb=bb, tt=t, tt_seq=t, ret_rows=t, tq=t, tm=min(b * t, 512), ffn_chunks=2)


def kernel(x_prompt, x_sample, state_conv, state_ret, cache_fox_k, cache_fox_v, cache_fox_logf, state_ffn_conv, g_mix, w_in, b_fgate, w_dw_conv, b_dw_conv, ln_conv_g, ln_conv_b, w_conv_out, gn_ret_g, w_ret_out, g_q_fox, g_k_fox, w_fox_out, w_out, g_ffn, w_ffn_up, w_ffn_dw, b_ffn_dw, w_ffn_down):
    bp, sp, _ = x_prompt.shape
    bs, ts, _ = x_sample.shape
    depth = w_in.shape[0]
    p_len = cache_fox_k.shape[2]
    conv_w = w_dw_conv.shape[1]
    ffn_w = w_ffn_dw.shape[1]
    d_conv = w_dw_conv.shape[2]
    d_ff = w_ffn_down.shape[1]
    assert sp >= conv_w - 1 and ts >= conv_w - 1 and conv_w - 1 <= HALO and ffn_w - 1 <= SUBLANES
    assert d_conv == SEG and w_in.shape[2] == 9 * SEG + N_HEADS + 3 * x_prompt.shape[2]
    pos_p = jnp.arange(sp)
    pos_s = p_len + jnp.arange(ts)
    cfg_p = _group_cfg(bp, sp)
    cfg_s = _group_cfg(bs, ts)
    weights = (g_mix, w_in, b_fgate, w_dw_conv, b_dw_conv, ln_conv_g, ln_conv_b, w_conv_out,
               gn_ret_g, w_ret_out, g_q_fox, g_k_fox, w_fox_out, w_out,
               g_ffn, w_ffn_up, w_ffn_dw, b_ffn_dw, w_ffn_down)
    xp, xs = x_prompt, x_sample
    st_p, st_s, fox_s = [], [], []
    stacked = None
    for l in range(depth):
        lw = tuple(w[l] for w in weights)
        xp, sp_l, (fkT_all, fvT_all, lfT) = _layer(
            xp, pos_p,
            jnp.zeros((bp, conv_w - 1, d_conv), F32),
            jnp.zeros((bp, N_HEADS, D_HEAD, D_HEAD), F32),
            None,
            jnp.zeros((bp, ffn_w - 1, d_ff), F32), lw, cfg_p, l, depth, stacked)
        stacked = (fkT_all, fvT_all)
        xs, ss_l, fs_l = _layer(xs, pos_s, state_conv[l], state_ret[l],
                                (cache_fox_k, cache_fox_v, cache_fox_logf),
                                state_ffn_conv[l], lw, cfg_s, l, depth, None)
        st_p.append(sp_l + (lfT,))
        st_s.append(ss_l)
        fox_s.append(fs_l)
    stack = lambda sts, i: jnp.stack([s[i] for s in sts])
    heads_last = lambda a: jnp.transpose(a.reshape(depth, bp, N_HEADS, D_HEAD, sp), (0, 1, 4, 2, 3))
    return (xp, xs,
            stack(st_p, 0), stack(st_p, 1), heads_last(stacked[0]), heads_last(stacked[1]),
            jnp.transpose(stack(st_p, 3), (0, 1, 3, 2)), stack(st_p, 2),
            stack(st_s, 0), stack(st_s, 1), stack(fox_s, 0), stack(fox_s, 1), stack(fox_s, 2), stack(st_s, 2))
```

```python
import functools
import math

import jax
import jax.numpy as jnp
import numpy as np
from jax import lax
from jax.experimental import pallas as pl
from jax.experimental.pallas import tpu as pltpu

F32 = jnp.float32
BF16 = jnp.bfloat16

EPS = 1e-6
ROPE_BASE = 10000.0
CHUNK = 64
N_HEADS = 8
D_HEAD = 64
SEG = N_HEADS * D_HEAD
LANES = 128
SUBLANES = 8
HALO = 32
VMEM_LIMIT = 56 * 1024 * 1024
LOG2E = math.log2(math.e)


def _cparams(sem):
    return pltpu.CompilerParams(dimension_semantics=sem, vmem_limit_bytes=VMEM_LIMIT)


def _dot(a, b):
    return jnp.dot(a, b, preferred_element_type=F32)


def _dot_nt(a, b):
    return lax.dot_general(a, b, (((1,), (1,)), ((), ())), preferred_element_type=F32)


def _dot_tn(a, b):
    return lax.dot_general(a, b, (((0,), (0,)), ((), ())), preferred_element_type=F32)


def _split3(x):
    hi = x.astype(BF16)
    r1 = x - hi.astype(F32)
    mid = r1.astype(BF16)
    lo = (r1 - mid.astype(F32)).astype(BF16)
    return hi, mid, lo


def _dot_exact_rhs(a_f32, b_bf16):
    hi, mid, lo = _split3(a_f32)
    return _dot(hi, b_bf16) + _dot(mid, b_bf16) + _dot(lo, b_bf16)


def _dot_exact_lhs(a_bf16, b_f32):
    hi, mid, lo = _split3(b_f32)
    return _dot(a_bf16, hi) + _dot(a_bf16, mid) + _dot(a_bf16, lo)


def _sigmoid(x):
    return 1.0 / (1.0 + jnp.exp(-x))


def _silu(x):
    return x * _sigmoid(x)


def _rms_rows(x, g):
    return x * lax.rsqrt(jnp.mean(x * x, axis=-1, keepdims=True) + EPS) * g


def _inproj_common(x_ref, g_ref, w_ref, cos_ref, sin_ref, u_ref, rq_ref, rk_ref, rv_ref, sg_ref):
    bb, tt, d = x_ref.shape
    tm = bb * tt
    x = x_ref[...].reshape(tm, d)
    h = _rms_rows(x, g_ref[...]).astype(BF16)

    def proj(i):
        return _dot(h, w_ref[:, i * SEG:(i + 1) * SEG])

    def put(ref, val):
        ref[...] = val.reshape(bb, tt, SEG).astype(ref.dtype)

    put(u_ref, proj(0) * _sigmoid(proj(1)))

    cos = cos_ref[...][None]
    sin = sin_ref[...][None]
    half = D_HEAD // 2
    lane = lax.broadcasted_iota(jnp.int32, (1, LANES), 1)
    first_half = (lane % D_HEAD) < half

    def rotary(p, scale):
        cols = []
        for c in range(SEG // LANES):
            pc = p[:, c * LANES:(c + 1) * LANES]
            partner = jnp.where(first_half, pltpu.roll(pc, LANES - half, 1), pltpu.roll(pc, half, 1))
            r = pc.reshape(bb, tt, LANES) * cos + partner.reshape(bb, tt, LANES) * sin
            cols.append(r * scale)
        return jnp.concatenate(cols, axis=-1)

    rq_ref[...] = rotary(proj(2), 1.0).astype(BF16)
    rk_ref[...] = rotary(proj(3), D_HEAD ** -0.5).astype(BF16)
    put(rv_ref, proj(4))
    put(sg_ref, _silu(proj(5)))
    return h, proj, put


def _head_rms(p, g, bd):
    ms = _dot((p * p).astype(BF16), bd)
    return p * lax.rsqrt(ms + EPS) * g


def _head_rms_t(z, gT_ref):
    tm = z.shape[1]
    z3 = z.reshape(N_HEADS, D_HEAD, tm)
    ms = jnp.mean(z3 * z3, axis=1, keepdims=True)
    g = jnp.concatenate([gT_ref[...]] * (tm // LANES), axis=1).reshape(N_HEADS, D_HEAD, tm)
    return (z3 * lax.rsqrt(ms + EPS) * g).reshape(SEG, tm)


def _log_forget(wff_ref, bff_ref, h):
    z = _dot_nt(wff_ref[...], h) + bff_ref[...]
    return jnp.minimum(z, 0.0) - jnp.log(1.0 + jnp.exp(-jnp.abs(z)))


def _inproj_seq_kernel(x_ref, g_ref, w_ref, wff_ref, bff_ref, cos_ref, sin_ref,
                       wT_ref, gqT_ref, gkT_ref, *rest):
    (u_ref, rq_ref, rk_ref, rv_ref, sg_ref, fqT_ref, fkb_ref, fvTb_ref, fkT_ref, fvT_ref, lf_ref) = rest[-11:]
    h, proj, put = _inproj_common(x_ref, g_ref, w_ref, cos_ref, sin_ref, u_ref, rq_ref, rk_ref, rv_ref, sg_ref)
    zq = _dot_nt(wT_ref[0:SEG, :], h)
    fqT_ref[0] = (_head_rms_t(zq, gqT_ref) * (D_HEAD ** -0.5 * LOG2E)).astype(BF16)
    zk = _dot_nt(wT_ref[SEG:2 * SEG, :], h)
    fkT = _head_rms_t(zk, gkT_ref)
    fkT_ref[0, 0] = fkT
    put(fkb_ref, jnp.transpose(fkT))
    fvT = _dot_nt(wT_ref[2 * SEG:3 * SEG, :], h)
    fvT_ref[0, 0] = fvT
    fvTb_ref[0] = fvT.astype(BF16)
    lf_ref[0] = _log_forget(wff_ref, bff_ref, h)


def _inproj_rows_kernel(x_ref, g_ref, w_ref, wff_ref, bff_ref, cos_ref, sin_ref, gq_ref, gk_ref, bd_ref,
                        u_ref, rq_ref, rk_ref, rv_ref, sg_ref, fq_ref, fk_ref, fkb_ref, fv_ref, fvb_ref, lf_ref):
    h, proj, put = _inproj_common(x_ref, g_ref, w_ref, cos_ref, sin_ref, u_ref, rq_ref, rk_ref, rv_ref, sg_ref)
    bd = bd_ref[...]
    put(fq_ref, _head_rms(proj(6), gq_ref[...], bd) * (D_HEAD ** -0.5))
    fk = _head_rms(proj(7), gk_ref[...], bd)
    put(fk_ref, fk)
    put(fkb_ref, fk)
    fv = proj(8)
    put(fv_ref, fv)
    put(fvb_ref, fv)
    lf_ref[0] = _log_forget(wff_ref, bff_ref, h)


def _inproj_seq(x, g, w_main, w_ffT, b_ff, cos_t, sin_t, wT, gqT, gkT, stacked, layer, depth, *, tt):
    b, t, d = x.shape
    act = lambda: pl.BlockSpec((1, tt, SEG), lambda i, j: (j, i, 0))
    act_t = lambda: pl.BlockSpec((1, SEG, tt), lambda i, j: (j, 0, i))
    stk = lambda: pl.BlockSpec((1, 1, SEG, tt), lambda i, j: (layer, j, 0, i))
    const = lambda shape: pl.BlockSpec(shape, lambda i, j: (0,) * len(shape))
    nat = lambda dt: jax.ShapeDtypeStruct((b, t, SEG), dt)
    tr = jax.ShapeDtypeStruct((b, SEG, t), BF16)
    stacked_shape = jax.ShapeDtypeStruct((depth, b, SEG, t), F32)
    out_shape = [nat(F32), nat(BF16), nat(BF16), nat(BF16), nat(BF16),
                 tr, nat(BF16), tr,
                 stacked_shape, stacked_shape,
                 jax.ShapeDtypeStruct((b, N_HEADS, t), F32)]
    out_specs = [act(), act(), act(), act(), act(), act_t(), act(), act_t(), stk(), stk(),
                 pl.BlockSpec((1, N_HEADS, tt), lambda i, j: (j, 0, i))]
    operands = [x, g, w_main, w_ffT, b_ff, cos_t, sin_t, wT, gqT, gkT]
    in_specs = [pl.BlockSpec((1, tt, d), lambda i, j: (j, i, 0)),
                const((1, d)), const(w_main.shape), const(w_ffT.shape), const((N_HEADS, 1)),
                pl.BlockSpec((tt, LANES), lambda i, j: (i, 0)), pl.BlockSpec((tt, LANES), lambda i, j: (i, 0)),
                const(wT.shape), const(gqT.shape), const(gkT.shape)]
    aliases = {}
    if stacked is not None:
        aliases = {len(operands): 8, len(operands) + 1: 9}
        operands += list(stacked)
        in_specs += [pl.BlockSpec(memory_space=pl.ANY)] * 2
    return pl.pallas_call(
        _inproj_seq_kernel,
        grid=(t // tt, b),
        in_specs=in_specs,
        out_specs=out_specs,
        out_shape=out_shape,
        input_output_aliases=aliases,
        compiler_params=_cparams(("arbitrary", "arbitrary")),
        name="inproj_seq",
    )(*operands)


def _inproj_rows(x, g, w_main, w_ffT, b_ff, cos_t, sin_t, gq, gk, bd, *, bb, tt):
    b, t, d = x.shape
    n_t, n_b = t // tt, b // bb
    tm = bb * tt
    act = lambda: pl.BlockSpec((bb, tt, SEG), lambda i, j: (j, i, 0))
    const = lambda shape: pl.BlockSpec(shape, lambda i, j: (0,) * len(shape))
    nat = lambda dt: jax.ShapeDtypeStruct((b, t, SEG), dt)
    out_shape = [nat(F32), nat(BF16), nat(BF16), nat(BF16), nat(BF16), nat(BF16),
                 nat(F32), nat(BF16), nat(F32), nat(BF16),
                 jax.ShapeDtypeStruct((n_t * n_b, N_HEADS, tm), F32)]
    out_specs = [act() for _ in range(10)] + [pl.BlockSpec((1, N_HEADS, tm), lambda i, j: (i * n_b + j, 0, 0))]
    outs = pl.pallas_call(
        _inproj_rows_kernel,
        grid=(n_t, n_b),
        in_specs=[pl.BlockSpec((bb, tt, d), lambda i, j: (j, i, 0)),
                  const((1, d)), const(w_main.shape), const(w_ffT.shape), const((N_HEADS, 1)),
                  pl.BlockSpec((tt, LANES), lambda i, j: (i, 0)), pl.BlockSpec((tt, LANES), lambda i, j: (i, 0)),
                  const((1, SEG)), const((1, SEG)), const((SEG, SEG))],
        out_specs=out_specs,
        out_shape=out_shape,
        compiler_params=_cparams(("arbitrary", "arbitrary")),
        name="inproj_rows",
    )(x, g, w_main, w_ffT, b_ff, cos_t, sin_t, gq, gk, bd)
    lf = outs[-1].reshape(n_t, n_b, N_HEADS, bb, tt)
    lf = jnp.transpose(lf, (1, 3, 2, 0, 4)).reshape(b, N_HEADS, t)
    return outs[:-1], lf


CONV_ROWS = 32


def _conv_context(st_ref, xbuf, tt, first_tile):
    @pl.when(first_tile)
    def _():
        xbuf[:, 0:HALO, :] = st_ref[...]

    @pl.when(jnp.logical_not(first_tile))
    def _():
        xbuf[:, 0:HALO, :] = xbuf[:, tt:tt + HALO, :]


def _conv_pieces(u_ref, w_ref, b_ref, lg_ref, lb_ref, a_out, xbuf, xsh):
    bb, tt, c = u_ref.shape
    width = w_ref.shape[0]
    rows = min(tt, CONV_ROWS)
    n_rows = HALO + tt
    first = HALO - (width - 1)

    def fill():
        xbuf[:, HALO:n_rows, :] = u_ref[...]
        xbuf[:, n_rows:n_rows + SUBLANES, :] = jnp.zeros((bb, SUBLANES, c), F32)

    def shift(s, s0):
        win = xbuf[s, s0:s0 + rows + SUBLANES, :]
        for sh in range(1, SUBLANES):
            xsh[sh - 1, s, s0:s0 + rows, :] = win[sh:sh + rows, :]

    def taps(s, r0):
        acc = jnp.zeros((rows, c), F32) + b_ref[...]
        for j in range(width):
            whole, sh = divmod(first + j, SUBLANES)
            lo = r0 + whole * SUBLANES
            tap = xbuf[s, lo:lo + rows, :] if sh == 0 else xsh[sh - 1, s, lo:lo + rows, :]
            acc = acc + w_ref[j:j + 1, :] * tap
        mu = jnp.mean(acc, axis=-1, keepdims=True)
        xc = acc - mu
        var = jnp.mean(xc * xc, axis=-1, keepdims=True)
        y = xc * lax.rsqrt(var + EPS) * lg_ref[...] + lb_ref[...]
        a_out[s * tt + r0:s * tt + r0 + rows, :] = _silu(y).astype(a_out.dtype)

    pieces = [fill]
    for s in range(bb):
        pieces += [functools.partial(shift, s, s0) for s0 in range(0, n_rows, rows)]
        pieces += [functools.partial(taps, s, r0) for r0 in range(0, tt, rows)]
    return pieces


def _ret_kernel(q_ref, k_ref, v_ref, sg_ref, st0_ref, dmat_ref, qdec_ref, kdec_ref, gn_ref, bd_ref,
                o_ref, st_ref, oall, *, tile_decay):
    t = pl.program_id(1)

    @pl.when(t == 0)
    def _():
        st_ref[...] = st0_ref[...]

    q = q_ref[0]
    k = k_ref[0]
    q_in = (q.astype(F32) * qdec_ref[...]).astype(BF16)
    k_end = (k.astype(F32) * kdec_ref[...]).astype(BF16)
    for h in range(N_HEADS):
        sl = slice(h * D_HEAD, (h + 1) * D_HEAD)
        qh, kh, vh = q[:, sl], k[:, sl], v_ref[0, :, sl]
        s = _dot_nt(qh, kh) * dmat_ref[h]
        state = st_ref[0, h]
        o = _dot(s.astype(BF16), vh) + _dot(q_in[:, sl], state.astype(BF16))
        st_ref[0, h] = state * tile_decay[h] + _dot_tn(k_end[:, sl], vh)
        oall[:, sl] = o
    o = oall[...]
    bd = bd_ref[...]
    o_hi = o.astype(BF16)
    o_lo = (o - o_hi.astype(F32)).astype(BF16)
    oc = o - (_dot(o_hi, bd) + _dot(o_lo, bd))
    var = _dot((oc * oc).astype(BF16), bd)
    o_ref[0] = (oc * lax.rsqrt(var + EPS) * gn_ref[...] * sg_ref[0].astype(F32)).astype(o_ref.dtype)


def _ret_tables(rows, chunk):
    lg = np.log1p(-np.exp2(-5.0 - np.arange(N_HEADS, dtype=np.float64)))
    idx = np.arange(rows, dtype=np.float64)
    dist = idx[:, None] - idx[None, :]
    visible = (idx[None, :] // chunk) <= (idx[:, None] // chunk)
    dmat = np.where(visible[None], np.exp(lg[:, None, None] * np.abs(dist)[None]), 0.0)
    qdec = np.repeat(np.exp(lg[None, :] * (idx[:, None] + 1.0)), D_HEAD, axis=1)
    kdec = np.repeat(np.exp(lg[None, :] * (rows - 1.0 - idx[:, None])), D_HEAD, axis=1)
    tile_decay = tuple(float(v) for v in np.exp(lg * rows))
    return (jnp.asarray(dmat, F32), jnp.asarray(qdec, F32), jnp.asarray(kdec, F32), tile_decay)


def _retention(q, k, v, sg, state0, gn_g, bd, *, rows):
    b, t, c = q.shape
    chunk = min(t, CHUNK)
    dmat, qdec, kdec, tile_decay = _ret_tables(rows, chunk)
    const = lambda shape: pl.BlockSpec(shape, lambda i, j: (0,) * len(shape))
    act = lambda: pl.BlockSpec((1, rows, c), lambda i, j: (i, j, 0))
    st_spec = lambda: pl.BlockSpec((1, N_HEADS, D_HEAD, D_HEAD), lambda i, j: (i, 0, 0, 0))
    return pl.pallas_call(
        functools.partial(_ret_kernel, tile_decay=tile_decay),
        grid=(b, t // rows),
        in_specs=[act(), act(), act(), act(), st_spec(),
                  const(dmat.shape), const(qdec.shape), const(kdec.shape), const((1, c)), const(bd.shape)],
        out_specs=[act(), st_spec()],
        out_shape=[jax.ShapeDtypeStruct((b, t, c), BF16),
                   jax.ShapeDtypeStruct((b, N_HEADS, D_HEAD, D_HEAD), F32)],
        scratch_shapes=[pltpu.VMEM((rows, c), F32)],
        compiler_params=_cparams(("arbitrary", "arbitrary")),
        name="ret",
    )(q, k, v, sg, state0, dmat, qdec, kdec, gn_g, bd)


CUM_BLOCK = 256
AUG_PIECES = 3


def _cumaug_kernel(lf_ref, o_ref):
    t = lf_ref.shape[1]
    blk = min(CUM_BLOCK, t)
    r = lax.broadcasted_iota(jnp.int32, (blk, blk), 0)
    c = lax.broadcasted_iota(jnp.int32, (blk, blk), 1)
    lower = (c <= r).astype(BF16)
    sh = lax.broadcasted_iota(jnp.int32, (N_HEADS, LANES), 0)
    sl = lax.broadcasted_iota(jnp.int32, (N_HEADS, LANES), 1)
    place = [(sl == AUG_PIECES * sh + p).astype(BF16) for p in range(AUG_PIECES)]
    carry = jnp.zeros((1, N_HEADS), F32)
    for i in range(t // blk):
        cs = _dot_exact_lhs(lower, lf_ref[0, i * blk:(i + 1) * blk, :]) + carry
        carry = cs[blk - 1:blk, :]
        pieces = _split3(-LOG2E * cs)
        aug = sum(_dot(piece, sel) for piece, sel in zip(pieces, place))
        o_ref[0, i * blk:(i + 1) * blk, :] = aug.astype(o_ref.dtype)


def _cumaug(logf):
    b, t, h = logf.shape
    return pl.pallas_call(
        _cumaug_kernel,
        grid=(b,),
        in_specs=[pl.BlockSpec((1, t, h), lambda i: (i, 0, 0))],
        out_specs=pl.BlockSpec((1, t, LANES), lambda i: (i, 0, 0)),
        out_shape=jax.ShapeDtypeStruct((b, t, LANES), BF16),
        compiler_params=_cparams(("arbitrary",)),
        name="cumaug",
    )(logf)


ONES_ROWS = 16


def _fox_kernel(qi_ref, kj_ref, qT_ref, k_ref, ca_ref, vT_ref, o_ref, qaug, m_ref, acc_ref, s0_ref, s1_ref):
    s = pl.program_id(1)
    i = qi_ref[s]
    j = kj_ref[s]
    tq = qT_ref.shape[2]
    tk = k_ref.shape[1]
    s_bufs = (s0_ref, s1_ref)

    @pl.when(j == 0)
    def _():
        row = lax.broadcasted_iota(jnp.int32, (LANES, tq), 0)
        for h in range(N_HEADS):
            qT = qT_ref[0, (h // 2) * LANES:(h // 2 + 1) * LANES, :]
            qaug[h, 0:LANES, :] = jnp.where(row // D_HEAD == h % 2, qT, jnp.zeros_like(qT))
            pick = (row >= AUG_PIECES * h) & (row < AUG_PIECES * (h + 1))
            qaug[h, LANES:2 * LANES, :] = jnp.where(pick, 1.0, 0.0).astype(BF16)
        m_ref[...] = jnp.full(m_ref.shape, -jnp.inf, F32)
        acc_ref[...] = jnp.zeros(acc_ref.shape, F32)

    def keys_block(h):
        return jnp.concatenate([k_ref[0, :, (h // 2) * LANES:(h // 2 + 1) * LANES], ca_ref[0]], axis=1)

    def full_step():
        ones = jnp.ones((ONES_ROWS, tk), BF16)

        def logits(h):
            s_bufs[h % 2][...] = _dot(keys_block(h), qaug[h])

        logits(0)
        for h in range(N_HEADS):
            if h + 1 < N_HEADS:
                logits(h + 1)
            sT = s_bufs[h % 2][...]
            m_old = m_ref[h]
            m_new = jnp.maximum(m_old, jnp.max(sT, axis=0, keepdims=True))
            alpha = jnp.exp2(m_old - m_new)
            p = jnp.exp2(sT - m_new).astype(BF16)
            lhs = jnp.concatenate([vT_ref[0, h * D_HEAD:(h + 1) * D_HEAD, :], ones], axis=0)
            acc_ref[h] = alpha * acc_ref[h] + _dot(lhs, p)
            m_ref[h] = m_new

    def diagonal_step():
        hk = tk // 2
        ones = jnp.ones((ONES_ROWS, hk), BF16)
        key = lax.broadcasted_iota(jnp.int32, (hk, tq), 0)
        qry = lax.broadcasted_iota(jnp.int32, (hk, tq), 1)
        visible_a = key <= qry
        visible_b = visible_a[:, :tq - hk]

        def logits(h):
            kk = keys_block(h)
            s_bufs[h % 2][0:hk, :] = _dot(kk[0:hk], qaug[h])
            s_bufs[h % 2][hk:tk, hk:tq] = _dot(kk[hk:tk], qaug[h, :, hk:tq])

        logits(0)
        for h in range(N_HEADS):
            if h + 1 < N_HEADS:
                logits(h + 1)
            s_a = jnp.where(visible_a, s_bufs[h % 2][0:hk, :], -jnp.inf)
            s_b = jnp.where(visible_b, s_bufs[h % 2][hk:tk, hk:tq], -jnp.inf)
            m_old = m_ref[h]
            top = jnp.max(s_a, axis=0, keepdims=True)
            top = jnp.concatenate([top[:, :hk], jnp.maximum(top[:, hk:], jnp.max(s_b, axis=0, keepdims=True))],
                                  axis=1)
            m_new = jnp.maximum(m_old, top)
            alpha = jnp.exp2(m_old - m_new)
            p_a = jnp.exp2(s_a - m_new).astype(BF16)
            p_b = jnp.exp2(s_b - m_new[:, hk:]).astype(BF16)
            vT = vT_ref[0, h * D_HEAD:(h + 1) * D_HEAD, :]
            upd_a = _dot(jnp.concatenate([vT[:, :hk], ones], axis=0), p_a)
            upd_b = _dot(jnp.concatenate([vT[:, hk:], ones], axis=0), p_b)
            upd = upd_a + jnp.concatenate([jnp.zeros((upd_b.shape[0], hk), F32), upd_b], axis=1)
            acc_ref[h] = alpha * acc_ref[h] + upd
            m_ref[h] = m_new

    @pl.when(j < i)
    def _():
        full_step()

    @pl.when(j == i)
    def _():
        diagonal_step()
        for hp in range(N_HEADS // 2):
            outs = []
            for h in (2 * hp, 2 * hp + 1):
                acc = acc_ref[h]
                outs.append(acc[0:D_HEAD, :] / acc[D_HEAD:D_HEAD + 1, :])
            o_ref[0, :, hp * LANES:(hp + 1) * LANES] = (
                jnp.transpose(jnp.concatenate(outs, axis=0)).astype(o_ref.dtype))


def _fox_prompt(qT, k, vT, caug, *, tq):
    b, c, t = qT.shape
    n_q = t // tq
    pairs = [(i, j) for i in range(n_q) for j in range(i + 1)]
    qi = jnp.asarray([p[0] for p in pairs], jnp.int32)
    kj = jnp.asarray([p[1] for p in pairs], jnp.int32)
    grid_spec = pltpu.PrefetchScalarGridSpec(
        num_scalar_prefetch=2,
        grid=(b, len(pairs)),
        in_specs=[pl.BlockSpec((1, c, tq), lambda bi, s, qi, kj: (bi, 0, qi[s])),
                  pl.BlockSpec((1, tq, c), lambda bi, s, qi, kj: (bi, kj[s], 0)),
                  pl.BlockSpec((1, tq, LANES), lambda bi, s, qi, kj: (bi, kj[s], 0)),
                  pl.BlockSpec((1, c, tq), lambda bi, s, qi, kj: (bi, 0, kj[s]))],
        out_specs=pl.BlockSpec((1, tq, c), lambda bi, s, qi, kj: (bi, qi[s], 0)),
        scratch_shapes=[pltpu.VMEM((N_HEADS, 2 * LANES, tq), BF16), pltpu.VMEM((N_HEADS, 1, tq), F32),
                        pltpu.VMEM((N_HEADS, D_HEAD + ONES_ROWS, tq), F32),
                        pltpu.VMEM((tq, tq), F32), pltpu.VMEM((tq, tq), F32)])
    return pl.pallas_call(
        _fox_kernel,
        grid_spec=grid_spec,
        out_shape=jax.ShapeDtypeStruct((b, t, c), BF16),
        compiler_params=_cparams(("arbitrary", "arbitrary")),
        name="fox",
    )(qi, kj, qT, k, caug, vT)


def _fox_cache_kernel(q_ref, kn_ref, vn_ref, kpT_ref, vpT_ref, lfp_ref, lfn_ref, o_ref, cum_ref):
    t = q_ref.shape[1]
    p_len = kpT_ref.shape[3]
    c = q_ref.shape[2]
    n_rows = N_HEADS * t

    r = lax.broadcasted_iota(jnp.int32, (LANES, LANES), 0)
    cc = lax.broadcasted_iota(jnp.int32, (LANES, LANES), 1)
    upper = (r <= cc).astype(BF16)
    carry = jnp.zeros((N_HEADS, 1), F32)
    for i in range(p_len // LANES):
        cs = _dot_exact_rhs(lfp_ref[0, 0, :, i * LANES:(i + 1) * LANES], upper) + carry
        cum_ref[:, i * LANES:(i + 1) * LANES] = cs
        carry = cs[:, LANES - 1:LANES]
    cum_new = _dot_exact_rhs(lfn_ref[0], upper[:t, :t]) + carry

    def per_query_rows(cum):
        n = cum.shape[1]
        return jnp.broadcast_to(cum[:, None, :], (N_HEADS, t, n)).reshape(n_rows, n)

    q = q_ref[0]
    qrep = jnp.concatenate([q] * N_HEADS, axis=0)
    qr = lax.broadcasted_iota(jnp.int32, (n_rows, c), 0)
    ql = lax.broadcasted_iota(jnp.int32, (n_rows, c), 1)
    qbd = jnp.where(qr // t == ql // D_HEAD, qrep, jnp.zeros_like(qrep))

    s_past = _dot(qbd, kpT_ref[0, 0].astype(BF16)) - per_query_rows(cum_ref[...])
    s_new = _dot_nt(qbd, kn_ref[0]) - per_query_rows(cum_new)
    qry = lax.broadcasted_iota(jnp.int32, (n_rows, t), 0) % t
    key = lax.broadcasted_iota(jnp.int32, (n_rows, t), 1)
    s_new = jnp.where(key <= qry, s_new, -jnp.inf)

    m = jnp.maximum(jnp.max(s_past, axis=1, keepdims=True), jnp.max(s_new, axis=1, keepdims=True))
    p_past = jnp.exp(s_past - m)
    p_new = jnp.exp(s_new - m)
    inv = 1.0 / (jnp.sum(p_past, axis=1, keepdims=True) + jnp.sum(p_new, axis=1, keepdims=True))
    o_full = (_dot_nt((p_past * inv).astype(BF16), vpT_ref[0, 0].astype(BF16))
              + _dot((p_new * inv).astype(BF16), vn_ref[0]))
    for h in range(N_HEADS):
        o_ref[0, :, h * D_HEAD:(h + 1) * D_HEAD] = (
            o_full[h * t:(h + 1) * t, h * D_HEAD:(h + 1) * D_HEAD].astype(o_ref.dtype))


def _fox_cache(q, kn, vn, kpT, vpT, lf_pastT, lf_newT, layer):
    b, t, c = q.shape
    p_len = kpT.shape[3]
    assert p_len % LANES == 0 and t <= LANES
    new = lambda: pl.BlockSpec((1, t, c), lambda i: (i, 0, 0))
    past = lambda: pl.BlockSpec((1, 1, c, p_len), lambda i: (layer, i, 0, 0))
    return pl.pallas_call(
        _fox_cache_kernel,
        grid=(b,),
        in_specs=[new(), new(), new(), past(), past(),
                  pl.BlockSpec((1, 1, N_HEADS, p_len), lambda i: (layer, i, 0, 0)),
                  pl.BlockSpec((1, N_HEADS, t), lambda i: (i, 0, 0))],
        out_specs=new(),
        out_shape=jax.ShapeDtypeStruct((b, t, c), BF16),
        scratch_shapes=[pltpu.VMEM((N_HEADS, p_len), F32)],
        compiler_params=_cparams(("arbitrary",)),
        name="fox_cache",
    )(q, kn, vn, kpT, vpT, lf_pastT, lf_newT)


def _merge_kernel(x_ref, g_ref, wg_ref, u_ref, st_ref, wdw_ref, bdw_ref, lg_ref, lb_ref, b_ref, c_ref,
                  wa_ref, wb_ref, wc_ref, wo_ref, o_ref, xbuf, xsh, a_br):
    bb, tt, d = x_ref.shape
    tm = bb * tt
    _conv_context(st_ref, xbuf, tt, pl.program_id(1) == 0)
    for piece in _conv_pieces(u_ref, wdw_ref, bdw_ref, lg_ref, lb_ref, a_br, xbuf, xsh):
        piece()
    x = x_ref[...].reshape(tm, d)
    h = _rms_rows(x, g_ref[...]).astype(BF16)
    branches = ((lambda: a_br[...], wa_ref), (lambda: b_ref[...].reshape(tm, SEG), wb_ref),
                (lambda: c_ref[...].reshape(tm, SEG), wc_ref))
    merged = jnp.zeros((tm, d), F32)
    for n, (branch, w_ref) in enumerate(branches):
        gate = _sigmoid(_dot(h, wg_ref[:, n * d:(n + 1) * d]))
        merged = merged + gate * _dot(branch(), w_ref[...])
    o_ref[...] = (x + _dot(merged.astype(BF16), wo_ref[...])).reshape(bb, tt, d)


def _merge(x, g, w_gates, u, conv_state, w_dw, b_dw, ln_g, ln_b, bm, c, wa, wb, wc, wo, *, bb, tt):
    b, t, d = x.shape
    width = w_dw.shape[0]
    st = jnp.pad(conv_state.astype(F32), ((0, 0), (HALO - (width - 1), 0), (0, 0)))
    const = lambda shape: pl.BlockSpec(shape, lambda i, j: (0,) * len(shape), pipeline_mode=pl.Buffered(1))
    act = lambda width: pl.BlockSpec((bb, tt, width), lambda i, j: (i, j, 0))
    return pl.pallas_call(
        _merge_kernel,
        grid=(b // bb, t // tt),
        in_specs=[act(d), const((1, d)), const(w_gates.shape), act(SEG),
                  pl.BlockSpec((bb, HALO, SEG), lambda i, j: (i, 0, 0)),
                  const(w_dw.shape), const((1, SEG)), const((1, SEG)), const((1, SEG)),
                  act(SEG), act(SEG),
                  const(wa.shape), const(wb.shape), const(wc.shape), const(wo.shape)],
        out_specs=act(d),
        out_shape=jax.ShapeDtypeStruct((b, t, d), F32),
        scratch_shapes=[pltpu.VMEM((bb, HALO + tt + SUBLANES, SEG), F32),
                        pltpu.VMEM((SUBLANES - 1, bb, HALO + tt, SEG), F32),
                        pltpu.VMEM((bb * tt, SEG), BF16)],
        compiler_params=_cparams(("arbitrary", "arbitrary")),
        name="merge",
    )(x, g, w_gates, u, st, w_dw, b_dw, ln_g, ln_b, bm, c, wa, wb, wc, wo)


def _erf(x):
    return lax.erf(x)


def _ffn_kernel(x_ref, g_ref, wup_ref, st_ref, wdw_ref, bdw_ref, wdn_ref, o_ref, buf_ref, abuf, *, n_chunks):
    bb, tt, d = x_ref.shape
    tm = bb * tt
    dff = wdn_ref.shape[0]
    cf = dff // n_chunks
    width = wdw_ref.shape[0]
    t = pl.program_id(1)
    x = x_ref[...].reshape(tm, d)
    h = _rms_rows(x, g_ref[...]).astype(BF16)

    @pl.when(t == 0)
    def _():
        abuf[:, 0:SUBLANES, :] = st_ref[...]

    @pl.when(t > 0)
    def _():
        abuf[:, 0:SUBLANES, :] = abuf[:, tt:tt + SUBLANES, :]

    out = x
    first = SUBLANES - (width - 1)
    for n in range(n_chunks):
        cs = slice(n * cf, (n + 1) * cf)
        abuf[:, SUBLANES:SUBLANES + tt, cs] = _dot(h, wup_ref[:, cs]).reshape(bb, tt, cf)
        gate = _dot(h, wup_ref[:, dff + n * cf:dff + (n + 1) * cf])
        conv = jnp.zeros((bb, tt, cf), F32) + bdw_ref[:, cs]
        for j in range(width):
            conv = conv + wdw_ref[j:j + 1, cs] * abuf[:, first + j:first + j + tt, cs]
        conv = conv.reshape(tm, cf)
        act = 0.5 * conv * (1.0 + _erf(conv * (2.0 ** -0.5))) * gate
        out = out + _dot(act.astype(BF16), wdn_ref[cs, :])
    o_ref[...] = out.reshape(bb, tt, d)
    buf_ref[...] = abuf[:, SUBLANES + tt - (width - 1):SUBLANES + tt, :]


def _ffn(x, g, w_up, state, w_dw, b_dw, w_down, *, bb, tt, n_chunks):
    b, t, d = x.shape
    dff = w_down.shape[0]
    width = w_dw.shape[0]
    st = jnp.pad(state.astype(F32), ((0, 0), (SUBLANES - (width - 1), 0), (0, 0)))
    const = lambda shape: pl.BlockSpec(shape, lambda i, j: (0,) * len(shape), pipeline_mode=pl.Buffered(1))
    return pl.pallas_call(
        functools.partial(_ffn_kernel, n_chunks=n_chunks),
        grid=(b // bb, t // tt),
        in_specs=[pl.BlockSpec((bb, tt, d), lambda i, j: (i, j, 0)),
                  const((1, d)), const(w_up.shape),
                  pl.BlockSpec((bb, SUBLANES, dff), lambda i, j: (i, 0, 0)),
                  const(w_dw.shape), const((1, dff)), const(w_down.shape)],
        out_specs=[pl.BlockSpec((bb, tt, d), lambda i, j: (i, j, 0)),
                   pl.BlockSpec((bb, width - 1, dff), lambda i, j: (i, 0, 0))],
        out_shape=[jax.ShapeDtypeStruct((b, t, d), F32),
                   jax.ShapeDtypeStruct((b, width - 1, dff), F32)],
        scratch_shapes=[pltpu.VMEM((bb, SUBLANES + tt, dff), F32)],
        compiler_params=_cparams(("arbitrary", "arbitrary")),
        name="ffn",
    )(x, g, w_up, st, w_dw, b_dw, w_down)


def _rope_tables(pos):
    half = D_HEAD // 2
    inv = jnp.exp(-math.log(ROPE_BASE) * jnp.arange(half, dtype=F32) / half)
    ang = pos.astype(F32)[:, None] * inv[None, :]
    cos, sin = jnp.cos(ang), jnp.sin(ang)
    cos_h = jnp.concatenate([cos, cos], axis=1)
    sin_h = jnp.concatenate([-sin, sin], axis=1)
    reps = LANES // D_HEAD
    return jnp.tile(cos_h, (1, reps)), jnp.tile(sin_h, (1, reps))


def _layer(x, pos, conv_state, ret_state, fox_past, ffn_state, lw, cfg, layer, depth, stacked):
    (g_mix, w_in, b_fgate, w_dw_conv, b_dw_conv, ln_conv_g, ln_conv_b, w_conv_out,
     gn_ret_g, w_ret_out, g_q_fox, g_k_fox, w_fox_out, w_out,
     g_ffn, w_ffn_up, w_ffn_dw, b_ffn_dw, w_ffn_down) = lw
    b, t, d = x.shape
    n_main = 9 * SEG
    row = lambda v: v.reshape(1, -1).astype(F32)

    w_main = w_in[:, :n_main].astype(BF16)
    w_ffT = jnp.transpose(w_in[:, n_main:n_main + N_HEADS]).astype(BF16)
    w_gates = w_in[:, n_main + N_HEADS:].astype(BF16)
    b_ff = b_fgate.reshape(N_HEADS, 1).astype(F32)
    cos_t, sin_t = _rope_tables(pos)
    head_id = np.arange(SEG) // D_HEAD
    bd = jnp.asarray((head_id[:, None] == head_id[None, :]) / D_HEAD, BF16)
    tile_head = lambda v: jnp.tile(v.astype(F32), N_HEADS).reshape(1, SEG)
    tile_head_t = lambda v: jnp.broadcast_to(jnp.tile(v.astype(F32), N_HEADS)[:, None], (SEG, LANES))

    if fox_past is None:
        wT = jnp.transpose(w_in[:, 6 * SEG:9 * SEG]).astype(BF16)
        (u, rq, rk, rv, sg, fqT, fkb, fvTb, fkT_all, fvT_all, lfT) = _inproj_seq(
            x, row(g_mix), w_main[:, :6 * SEG], w_ffT, b_ff, cos_t, sin_t,
            wT, tile_head_t(g_q_fox), tile_head_t(g_k_fox), stacked, layer, depth, tt=cfg["tt"])
        fox_state = (fkT_all, fvT_all, lfT)
        caug = _cumaug(jnp.transpose(lfT, (0, 2, 1)))
        c_br = _fox_prompt(fqT, fkb, fvTb, caug, tq=cfg["tq"])
    else:
        (u, rq, rk, rv, sg, fq, fk, fkb, fv, fvb), lfT = _inproj_rows(
            x, row(g_mix), w_main, w_ffT, b_ff, cos_t, sin_t,
            tile_head(g_q_fox), tile_head(g_k_fox), bd, bb=cfg["bb"], tt=cfg["tt"])
        fox_state = (fk.reshape(b, t, N_HEADS, D_HEAD), fv.reshape(b, t, N_HEADS, D_HEAD),
                     jnp.transpose(lfT, (0, 2, 1)))
        k_all, v_all, logf_all = fox_past
        p_len = k_all.shape[2]
        time_last = lambda a: jnp.transpose(a, (0, 1, 3, 4, 2)).reshape(depth, b, SEG, p_len)
        c_br = _fox_cache(fq, fkb, fvb, time_last(k_all), time_last(v_all),
                          jnp.transpose(logf_all.astype(F32), (0, 1, 3, 2)), lfT, layer)

    new_conv = u[:, t - (w_dw_conv.shape[0] - 1):, :]

    b_br, new_ret = _retention(rq, rk, rv, sg, ret_state.astype(F32), row(gn_ret_g), bd, rows=cfg["ret_rows"])

    x1 = _merge(x, row(g_mix), w_gates, u, conv_state, w_dw_conv.astype(F32), row(b_dw_conv),
                row(ln_conv_g), row(ln_conv_b), b_br, c_br, w_conv_out.astype(BF16), w_ret_out.astype(BF16),
                w_fox_out.astype(BF16), w_out.astype(BF16), bb=cfg["bb"], tt=cfg["tt_seq"])

    x2, new_ffn = _ffn(x1, row(g_ffn), w_ffn_up.astype(BF16), ffn_state, w_ffn_dw.astype(F32), row(b_ffn_dw),
                       w_ffn_down.astype(BF16), bb=cfg["bb"], tt=cfg["tt_seq"], n_chunks=cfg["ffn_chunks"])
    return x2, (new_conv, new_ret, new_ffn), fox_state


def _group_cfg(b, t):
    if t >= 512:
        return dict(bb=1, tt=512, tt_seq=512, ret_rows=256, tq=1024 if t % 1024 == 0 else 512, tm=512,
                    ffn_chunks=2)
    bb = max(1, min(b, 512 // t))
    return dict(bb=bb, tt=t, tt_seq=t, ret_rows=t, tq=t, tm=min(b * t, 512), ffn_chunks=2)


def kernel(x_prompt, x_sample, state_conv, state_ret, cache_fox_k, cache_fox_v, cache_fox_logf, state_ffn_conv, g_mix, w_in, b_fgate, w_dw_conv, b_dw_conv, ln_conv_g, ln_conv_b, w_conv_out, gn_ret_g, w_ret_out, g_q_fox, g_k_fox, w_fox_out, w_out, g_ffn, w_ffn_up, w_ffn_dw, b_ffn_dw, w_ffn_down):
    bp, sp, _ = x_prompt.shape
    bs, ts, _ = x_sample.shape
    depth = w_in.shape[0]
    p_len = cache_fox_k.shape[2]
    conv_w = w_dw_conv.shape[1]
    ffn_w = w_ffn_dw.shape[1]
    d_conv = w_dw_conv.shape[2]
    d_ff = w_ffn_down.shape[1]
    assert sp >= conv_w - 1 and ts >= conv_w - 1 and conv_w - 1 <= HALO and ffn_w - 1 <= SUBLANES
    assert d_conv == SEG and w_in.shape[2] == 9 * SEG + N_HEADS + 3 * x_prompt.shape[2]
    pos_p = jnp.arange(sp)
    pos_s = p_len + jnp.arange(ts)
    cfg_p = _group_cfg(bp, sp)
    cfg_s = _group_cfg(bs, ts)
    weights = (g_mix, w_in, b_fgate, w_dw_conv, b_dw_conv, ln_conv_g, ln_conv_b, w_conv_out,
               gn_ret_g, w_ret_out, g_q_fox, g_k_fox, w_fox_out, w_out,
               g_ffn, w_ffn_up, w_ffn_dw, b_ffn_dw, w_ffn_down)
    xp, xs = x_prompt, x_sample
    st_p, st_s, fox_s = [], [], []
    stacked = None
    for l in range(depth):
        lw = tuple(w[l] for w in weights)
        xp, sp_l, (fkT_all, fvT_all, lfT) = _layer(
            xp, pos_p,
            jnp.zeros((bp, conv_w - 1, d_conv), F32),
            jnp.zeros((bp, N_HEADS, D_HEAD, D_HEAD), F32),
            None,
            jnp.zeros((bp, ffn_w - 1, d_ff), F32), lw, cfg_p, l, depth, stacked)
        stacked = (fkT_all, fvT_all)
        xs, ss_l, fs_l = _layer(xs, pos_s, state_conv[l], state_ret[l],
                                (cache_fox_k, cache_fox_v, cache_fox_logf),
                                state_ffn_conv[l], lw, cfg_s, l, depth, None)
        st_p.append(sp_l + (lfT,))
        st_s.append(ss_l)
        fox_s.append(fs_l)
    stack = lambda sts, i: jnp.stack([s[i] for s in sts])
    heads_last = lambda a: jnp.transpose(a.reshape(depth, bp, N_HEADS, D_HEAD, sp), (0, 1, 4, 2, 3))
    return (xp, xs,
            stack(st_p, 0), stack(st_p, 1), heads_last(stacked[0]), heads_last(stacked[1]),
            jnp.transpose(stack(st_p, 3), (0, 1, 3, 2)), stack(st_p, 2),
            stack(st_s, 0), stack(st_s, 1), stack(fox_s, 0), stack(fox_s, 1), stack(fox_s, 2), stack(st_s, 2))
```

```python
import functools
import math

import jax
import jax.numpy as jnp
import numpy as np
from jax import lax
from jax.experimental import pallas as pl
from jax.experimental.pallas import tpu as pltpu

F32 = jnp.float32
BF16 = jnp.bfloat16

EPS = 1e-6
ROPE_BASE = 10000.0
CHUNK = 64
N_HEADS = 8
D_HEAD = 64
SEG = N_HEADS * D_HEAD
LANES = 128
SUBLANES = 8
HALO = 32
VMEM_LIMIT = 56 * 1024 * 1024
LOG2E = math.log2(math.e)


def _cparams(sem):
    return pltpu.CompilerParams(dimension_semantics=sem, vmem_limit_bytes=VMEM_LIMIT)


def _dot(a, b):
    return jnp.dot(a, b, preferred_element_type=F32)


def _dot_nt(a, b):
    return lax.dot_general(a, b, (((1,), (1,)), ((), ())), preferred_element_type=F32)


def _dot_tn(a, b):
    return lax.dot_general(a, b, (((0,), (0,)), ((), ())), preferred_element_type=F32)


def _split3(x):
    hi = x.astype(BF16)
    r1 = x - hi.astype(F32)
    mid = r1.astype(BF16)
    lo = (r1 - mid.astype(F32)).astype(BF16)
    return hi, mid, lo


def _dot_exact_rhs(a_f32, b_bf16):
    hi, mid, lo = _split3(a_f32)
    return _dot(hi, b_bf16) + _dot(mid, b_bf16) + _dot(lo, b_bf16)


def _dot_exact_lhs(a_bf16, b_f32):
    hi, mid, lo = _split3(b_f32)
    return _dot(a_bf16, hi) + _dot(a_bf16, mid) + _dot(a_bf16, lo)


def _sigmoid(x):
    return 1.0 / (1.0 + jnp.exp(-x))


def _silu(x):
    return x * _sigmoid(x)


def _rms_rows(x, g):
    return x * lax.rsqrt(jnp.mean(x * x, axis=-1, keepdims=True) + EPS) * g


def _inproj_common(x_ref, g_ref, w_ref, cos_ref, sin_ref, u_ref, rq_ref, rk_ref, rv_ref, sg_ref):
    bb, tt, d = x_ref.shape
    tm = bb * tt
    x = x_ref[...].reshape(tm, d)
    h = _rms_rows(x, g_ref[...]).astype(BF16)

    def proj(i):
        return _dot(h, w_ref[:, i * SEG:(i + 1) * SEG])

    def put(ref, val):
        ref[...] = val.reshape(bb, tt, SEG).astype(ref.dtype)

    put(u_ref, proj(0) * _sigmoid(proj(1)))

    cos = cos_ref[...][None]
    sin = sin_ref[...][None]
    half = D_HEAD // 2
    lane = lax.broadcasted_iota(jnp.int32, (1, LANES), 1)
    first_half = (lane % D_HEAD) < half

    def rotary(p, scale):
        cols = []
        for c in range(SEG // LANES):
            pc = p[:, c * LANES:(c + 1) * LANES]
            partner = jnp.where(first_half, pltpu.roll(pc, LANES - half, 1), pltpu.roll(pc, half, 1))
            r = pc.reshape(bb, tt, LANES) * cos + partner.reshape(bb, tt, LANES) * sin
            cols.append(r * scale)
        return jnp.concatenate(cols, axis=-1)

    rq_ref[...] = rotary(proj(2), 1.0).astype(BF16)
    rk_ref[...] = rotary(proj(3), D_HEAD ** -0.5).astype(BF16)
    put(rv_ref, proj(4))
    put(sg_ref, _silu(proj(5)))
    return h, proj, put


def _head_rms(p, g, bd):
    ms = _dot((p * p).astype(BF16), bd)
    return p * lax.rsqrt(ms + EPS) * g


def _head_rms_t(z, gT_ref):
    tm = z.shape[1]
    z3 = z.reshape(N_HEADS, D_HEAD, tm)
    ms = jnp.mean(z3 * z3, axis=1, keepdims=True)
    g = jnp.concatenate([gT_ref[...]] * (tm // LANES), axis=1).reshape(N_HEADS, D_HEAD, tm)
    return (z3 * lax.rsqrt(ms + EPS) * g).reshape(SEG, tm)


def _log_forget(wff_ref, bff_ref, h):
    z = _dot_nt(wff_ref[...], h) + bff_ref[...]
    return jnp.minimum(z, 0.0) - jnp.log(1.0 + jnp.exp(-jnp.abs(z)))


def _inproj_seq_kernel(x_ref, g_ref, w_ref, wff_ref, bff_ref, cos_ref, sin_ref,
                       wT_ref, gqT_ref, gkT_ref, *rest):
    (u_ref, rq_ref, rk_ref, rv_ref, sg_ref, fqT_ref, fkb_ref, fvTb_ref, fkT_ref, fvT_ref, lf_ref) = rest[-11:]
    h, proj, put = _inproj_common(x_ref, g_ref, w_ref, cos_ref, sin_ref, u_ref, rq_ref, rk_ref, rv_ref, sg_ref)
    zq = _dot_nt(wT_ref[0:SEG, :], h)
    fqT_ref[0] = (_head_rms_t(zq, gqT_ref) * (D_HEAD ** -0.5 * LOG2E)).astype(BF16)
    zk = _dot_nt(wT_ref[SEG:2 * SEG, :], h)
    fkT = _head_rms_t(zk, gkT_ref)
    fkT_ref[0, 0] = fkT
    put(fkb_ref, jnp.transpose(fkT))
    fvT = _dot_nt(wT_ref[2 * SEG:3 * SEG, :], h)
    fvT_ref[0, 0] = fvT
    fvTb_ref[0] = fvT.astype(BF16)
    lf_ref[0] = _log_forget(wff_ref, bff_ref, h)


def _inproj_rows_kernel(x_ref, g_ref, w_ref, wff_ref, bff_ref, cos_ref, sin_ref, gq_ref, gk_ref, bd_ref,
                        u_ref, rq_ref, rk_ref, rv_ref, sg_ref, fq_ref, fk_ref, fkb_ref, fv_ref, fvb_ref, lf_ref):
    h, proj, put = _inproj_common(x_ref, g_ref, w_ref, cos_ref, sin_ref, u_ref, rq_ref, rk_ref, rv_ref, sg_ref)
    bd = bd_ref[...]
    put(fq_ref, _head_rms(proj(6), gq_ref[...], bd) * (D_HEAD ** -0.5))
    fk = _head_rms(proj(7), gk_ref[...], bd)
    put(fk_ref, fk)
    put(fkb_ref, fk)
    fv = proj(8)
    put(fv_ref, fv)
    put(fvb_ref, fv)
    lf_ref[0] = _log_forget(wff_ref, bff_ref, h)


def _inproj_seq(x, g, w_main, w_ffT, b_ff, cos_t, sin_t, wT, gqT, gkT, stacked, layer, depth, *, tt):
    b, t, d = x.shape
    act = lambda: pl.BlockSpec((1, tt, SEG), lambda i, j: (j, i, 0))
    act_t = lambda: pl.BlockSpec((1, SEG, tt), lambda i, j: (j, 0, i))
    stk = lambda: pl.BlockSpec((1, 1, SEG, tt), lambda i, j: (layer, j, 0, i))
    const = lambda shape: pl.BlockSpec(shape, lambda i, j: (0,) * len(shape))
    nat = lambda dt: jax.ShapeDtypeStruct((b, t, SEG), dt)
    tr = jax.ShapeDtypeStruct((b, SEG, t), BF16)
    stacked_shape = jax.ShapeDtypeStruct((depth, b, SEG, t), F32)
    out_shape = [nat(F32), nat(BF16), nat(BF16), nat(BF16), nat(BF16),
                 tr, nat(BF16), tr,
                 stacked_shape, stacked_shape,
                 jax.ShapeDtypeStruct((b, N_HEADS, t), F32)]
    out_specs = [act(), act(), act(), act(), act(), act_t(), act(), act_t(), stk(), stk(),
                 pl.BlockSpec((1, N_HEADS, tt), lambda i, j: (j, 0, i))]
    operands = [x, g, w_main, w_ffT, b_ff, cos_t, sin_t, wT, gqT, gkT]
    in_specs = [pl.BlockSpec((1, tt, d), lambda i, j: (j, i, 0)),
                const((1, d)), const(w_main.shape), const(w_ffT.shape), const((N_HEADS, 1)),
                pl.BlockSpec((tt, LANES), lambda i, j: (i, 0)), pl.BlockSpec((tt, LANES), lambda i, j: (i, 0)),
                const(wT.shape), const(gqT.shape), const(gkT.shape)]
    aliases = {}
    if stacked is not None:
        aliases = {len(operands): 8, len(operands) + 1: 9}
        operands += list(stacked)
        in_specs += [pl.BlockSpec(memory_space=pl.ANY)] * 2
    return pl.pallas_call(
        _inproj_seq_kernel,
        grid=(t // tt, b),
        in_specs=in_specs,
        out_specs=out_specs,
        out_shape=out_shape,
        input_output_aliases=aliases,
        compiler_params=_cparams(("arbitrary", "arbitrary")),
        name="inproj_seq",
    )(*operands)


def _inproj_rows(x, g, w_main, w_ffT, b_ff, cos_t, sin_t, gq, gk, bd, *, bb, tt):
    b, t, d = x.shape
    n_t, n_b = t // tt, b // bb
    tm = bb * tt
    act = lambda: pl.BlockSpec((bb, tt, SEG), lambda i, j: (j, i, 0))
    const = lambda shape: pl.BlockSpec(shape, lambda i, j: (0,) * len(shape))
    nat = lambda dt: jax.ShapeDtypeStruct((b, t, SEG), dt)
    out_shape = [nat(F32), nat(BF16), nat(BF16), nat(BF16), nat(BF16), nat(BF16),
                 nat(F32), nat(BF16), nat(F32), nat(BF16),
                 jax.ShapeDtypeStruct((n_t * n_b, N_HEADS, tm), F32)]
    out_specs = [act() for _ in range(10)] + [pl.BlockSpec((1, N_HEADS, tm), lambda i, j: (i * n_b + j, 0, 0))]
    outs = pl.pallas_call(
        _inproj_rows_kernel,
        grid=(n_t, n_b),
        in_specs=[pl.BlockSpec((bb, tt, d), lambda i, j: (j, i, 0)),
                  const((1, d)), const(w_main.shape), const(w_ffT.shape), const((N_HEADS, 1)),
                  pl.BlockSpec((tt, LANES), lambda i, j: (i, 0)), pl.BlockSpec((tt, LANES), lambda i, j: (i, 0)),
                  const((1, SEG)), const((1, SEG)), const((SEG, SEG))],
        out_specs=out_specs,
        out_shape=out_shape,
        compiler_params=_cparams(("arbitrary", "arbitrary")),
        name="inproj_rows",
    )(x, g, w_main, w_ffT, b_ff, cos_t, sin_t, gq, gk, bd)
    lf = outs[-1].reshape(n_t, n_b, N_HEADS, bb, tt)
    lf = jnp.transpose(lf, (1, 3, 2, 0, 4)).reshape(b, N_HEADS, t)
    return outs[:-1], lf


CONV_ROWS = 32


def _conv_context(st_ref, xbuf, tt, first_tile):
    @pl.when(first_tile)
    def _():
        xbuf[:, 0:HALO, :] = st_ref[...]

    @pl.when(jnp.logical_not(first_tile))
    def _():
        xbuf[:, 0:HALO, :] = xbuf[:, tt:tt + HALO, :]


def _conv_pieces(u_ref, w_ref, b_ref, lg_ref, lb_ref, a_out, xbuf, xsh):
    bb, tt, c = u_ref.shape
    width = w_ref.shape[0]
    rows = min(tt, CONV_ROWS)
    n_rows = HALO + tt
    first = HALO - (width - 1)

    def fill():
        xbuf[:, HALO:n_rows, :] = u_ref[...]
        xbuf[:, n_rows:n_rows + SUBLANES, :] = jnp.zeros((bb, SUBLANES, c), F32)

    def shift(s, s0):
        win = xbuf[s, s0:s0 + rows + SUBLANES, :]
        for sh in range(1, SUBLANES):
            xsh[sh - 1, s, s0:s0 + rows, :] = win[sh:sh + rows, :]

    def taps(s, r0):
        acc = jnp.zeros((rows, c), F32) + b_ref[...]
        for j in range(width):
            whole, sh = divmod(first + j, SUBLANES)
            lo = r0 + whole * SUBLANES
            tap = xbuf[s, lo:lo + rows, :] if sh == 0 else xsh[sh - 1, s, lo:lo + rows, :]
            acc = acc + w_ref[j:j + 1, :] * tap
        mu = jnp.mean(acc, axis=-1, keepdims=True)
        xc = acc - mu
        var = jnp.mean(xc * xc, axis=-1, keepdims=True)
        y = xc * lax.rsqrt(var + EPS) * lg_ref[...] + lb_ref[...]
        a_out[s * tt + r0:s * tt + r0 + rows, :] = _silu(y).astype(a_out.dtype)
        return y[0:SUBLANES, 0:LANES]

    pieces = [fill]
    for s in range(bb):
        pieces += [functools.partial(shift, s, 0), functools.partial(shift, s, rows)]
        for r0 in range(0, tt, rows):
            pieces.append(functools.partial(taps, s, r0))
            if r0 + 2 * rows < n_rows:
                pieces.append(functools.partial(shift, s, r0 + 2 * rows))
    return pieces


def _ret_kernel(q_ref, k_ref, v_ref, sg_ref, st0_ref, dmat_ref, qdec_ref, kdec_ref, gn_ref, bd_ref,
                o_ref, st_ref, oall, *, tile_decay):
    t = pl.program_id(1)

    @pl.when(t == 0)
    def _():
        st_ref[...] = st0_ref[...]

    q = q_ref[0]
    k = k_ref[0]
    q_in = (q.astype(F32) * qdec_ref[...]).astype(BF16)
    k_end = (k.astype(F32) * kdec_ref[...]).astype(BF16)
    for h in range(N_HEADS):
        sl = slice(h * D_HEAD, (h + 1) * D_HEAD)
        qh, kh, vh = q[:, sl], k[:, sl], v_ref[0, :, sl]
        s = _dot_nt(qh, kh) * dmat_ref[h]
        state = st_ref[0, h]
        o = _dot(s.astype(BF16), vh) + _dot(q_in[:, sl], state.astype(BF16))
        st_ref[0, h] = state * tile_decay[h] + _dot_tn(k_end[:, sl], vh)
        oall[:, sl] = o
    o = oall[...]
    bd = bd_ref[...]
    o_hi = o.astype(BF16)
    o_lo = (o - o_hi.astype(F32)).astype(BF16)
    oc = o - (_dot(o_hi, bd) + _dot(o_lo, bd))
    var = _dot((oc * oc).astype(BF16), bd)
    o_ref[0] = (oc * lax.rsqrt(var + EPS) * gn_ref[...] * sg_ref[0].astype(F32)).astype(o_ref.dtype)


def _ret_tables(rows, chunk):
    lg = np.log1p(-np.exp2(-5.0 - np.arange(N_HEADS, dtype=np.float64)))
    idx = np.arange(rows, dtype=np.float64)
    dist = idx[:, None] - idx[None, :]
    visible = (idx[None, :] // chunk) <= (idx[:, None] // chunk)
    dmat = np.where(visible[None], np.exp(lg[:, None, None] * np.abs(dist)[None]), 0.0)
    qdec = np.repeat(np.exp(lg[None, :] * (idx[:, None] + 1.0)), D_HEAD, axis=1)
    kdec = np.repeat(np.exp(lg[None, :] * (rows - 1.0 - idx[:, None])), D_HEAD, axis=1)
    tile_decay = tuple(float(v) for v in np.exp(lg * rows))
    return (jnp.asarray(dmat, F32), jnp.asarray(qdec, F32), jnp.asarray(kdec, F32), tile_decay)


def _retention(q, k, v, sg, state0, gn_g, bd, *, rows):
    b, t, c = q.shape
    chunk = min(t, CHUNK)
    dmat, qdec, kdec, tile_decay = _ret_tables(rows, chunk)
    const = lambda shape: pl.BlockSpec(shape, lambda i, j: (0,) * len(shape))
    act = lambda: pl.BlockSpec((1, rows, c), lambda i, j: (i, j, 0))
    st_spec = lambda: pl.BlockSpec((1, N_HEADS, D_HEAD, D_HEAD), lambda i, j: (i, 0, 0, 0))
    return pl.pallas_call(
        functools.partial(_ret_kernel, tile_decay=tile_decay),
        grid=(b, t // rows),
        in_specs=[act(), act(), act(), act(), st_spec(),
                  const(dmat.shape), const(qdec.shape), const(kdec.shape), const((1, c)), const(bd.shape)],
        out_specs=[act(), st_spec()],
        out_shape=[jax.ShapeDtypeStruct((b, t, c), BF16),
                   jax.ShapeDtypeStruct((b, N_HEADS, D_HEAD, D_HEAD), F32)],
        scratch_shapes=[pltpu.VMEM((rows, c), F32)],
        compiler_params=_cparams(("arbitrary", "arbitrary")),
        name="ret",
    )(q, k, v, sg, state0, dmat, qdec, kdec, gn_g, bd)


CUM_BLOCK = 256
AUG_PIECES = 3


def _cumaug_kernel(lf_ref, o_ref):
    t = lf_ref.shape[1]
    blk = min(CUM_BLOCK, t)
    r = lax.broadcasted_iota(jnp.int32, (blk, blk), 0)
    c = lax.broadcasted_iota(jnp.int32, (blk, blk), 1)
    lower = (c <= r).astype(BF16)
    sh = lax.broadcasted_iota(jnp.int32, (N_HEADS, LANES), 0)
    sl = lax.broadcasted_iota(jnp.int32, (N_HEADS, LANES), 1)
    place = [(sl == AUG_PIECES * sh + p).astype(BF16) for p in range(AUG_PIECES)]
    carry = jnp.zeros((1, N_HEADS), F32)
    for i in range(t // blk):
        cs = _dot_exact_lhs(lower, lf_ref[0, i * blk:(i + 1) * blk, :]) + carry
        carry = cs[blk - 1:blk, :]
        pieces = _split3(-LOG2E * cs)
        aug = sum(_dot(piece, sel) for piece, sel in zip(pieces, place))
        o_ref[0, i * blk:(i + 1) * blk, :] = aug.astype(o_ref.dtype)


def _cumaug(logf):
    b, t, h = logf.shape
    return pl.pallas_call(
        _cumaug_kernel,
        grid=(b,),
        in_specs=[pl.BlockSpec((1, t, h), lambda i: (i, 0, 0))],
        out_specs=pl.BlockSpec((1, t, LANES), lambda i: (i, 0, 0)),
        out_shape=jax.ShapeDtypeStruct((b, t, LANES), BF16),
        compiler_params=_cparams(("arbitrary",)),
        name="cumaug",
    )(logf)


ONES_ROWS = 16


def _fox_kernel(qi_ref, kj_ref, qT_ref, k_ref, ca_ref, vT_ref, o_ref, qaug, m_ref, acc_ref, s0_ref, s1_ref):
    s = pl.program_id(1)
    i = qi_ref[s]
    j = kj_ref[s]
    tq = qT_ref.shape[2]
    tk = k_ref.shape[1]
    s_bufs = (s0_ref, s1_ref)

    @pl.when(j == 0)
    def _():
        row = lax.broadcasted_iota(jnp.int32, (LANES, tq), 0)
        for h in range(N_HEADS):
            qT = qT_ref[0, (h // 2) * LANES:(h // 2 + 1) * LANES, :]
            qaug[h, 0:LANES, :] = jnp.where(row // D_HEAD == h % 2, qT, jnp.zeros_like(qT))
            pick = (row >= AUG_PIECES * h) & (row < AUG_PIECES * (h + 1))
            qaug[h, LANES:2 * LANES, :] = jnp.where(pick, 1.0, 0.0).astype(BF16)
        m_ref[...] = jnp.full(m_ref.shape, -jnp.inf, F32)
        acc_ref[...] = jnp.zeros(acc_ref.shape, F32)

    def keys_block(h):
        return jnp.concatenate([k_ref[0, :, (h // 2) * LANES:(h // 2 + 1) * LANES], ca_ref[0]], axis=1)

    def full_step():
        ones = jnp.ones((ONES_ROWS, tk), BF16)

        def logits(h):
            s_bufs[h % 2][...] = _dot(keys_block(h), qaug[h])

        logits(0)
        for h in range(N_HEADS):
            if h + 1 < N_HEADS:
                logits(h + 1)
            sT = s_bufs[h % 2][...]
            m_old = m_ref[h]
            m_new = jnp.maximum(m_old, jnp.max(sT, axis=0, keepdims=True))
            alpha = jnp.exp2(m_old - m_new)
            p = jnp.exp2(sT - m_new).astype(BF16)
            lhs = jnp.concatenate([vT_ref[0, h * D_HEAD:(h + 1) * D_HEAD, :], ones], axis=0)
            acc_ref[h] = alpha * acc_ref[h] + _dot(lhs, p)
            m_ref[h] = m_new

    def diagonal_step():
        hk = tk // 2
        ones = jnp.ones((ONES_ROWS, hk), BF16)
        key = lax.broadcasted_iota(jnp.int32, (hk, tq), 0)
        qry = lax.broadcasted_iota(jnp.int32, (hk, tq), 1)
        visible_a = key <= qry
        visible_b = visible_a[:, :tq - hk]

        def logits(h):
            kk = keys_block(h)
            s_bufs[h % 2][0:hk, :] = _dot(kk[0:hk], qaug[h])
            s_bufs[h % 2][hk:tk, hk:tq] = _dot(kk[hk:tk], qaug[h, :, hk:tq])

        logits(0)
        for h in range(N_HEADS):
            if h + 1 < N_HEADS:
                logits(h + 1)
            s_a = jnp.where(visible_a, s_bufs[h % 2][0:hk, :], -jnp.inf)
            s_b = jnp.where(visible_b, s_bufs[h % 2][hk:tk, hk:tq], -jnp.inf)
            m_old = m_ref[h]
            top = jnp.max(s_a, axis=0, keepdims=True)
            top = jnp.concatenate([top[:, :hk], jnp.maximum(top[:, hk:], jnp.max(s_b, axis=0, keepdims=True))],
                                  axis=1)
            m_new = jnp.maximum(m_old, top)
            alpha = jnp.exp2(m_old - m_new)
            p_a = jnp.exp2(s_a - m_new).astype(BF16)
            p_b = jnp.exp2(s_b - m_new[:, hk:]).astype(BF16)
            vT = vT_ref[0, h * D_HEAD:(h + 1) * D_HEAD, :]
            upd_a = _dot(jnp.concatenate([vT[:, :hk], ones], axis=0), p_a)
            upd_b = _dot(jnp.concatenate([vT[:, hk:], ones], axis=0), p_b)
            upd = upd_a + jnp.concatenate([jnp.zeros((upd_b.shape[0], hk), F32), upd_b], axis=1)
            acc_ref[h] = alpha * acc_ref[h] + upd
            m_ref[h] = m_new

    @pl.when(j < i)
    def _():
        full_step()

    @pl.when(j == i)
    def _():
        diagonal_step()
        for hp in range(N_HEADS // 2):
            outs = []
            for h in (2 * hp, 2 * hp + 1):
                acc = acc_ref[h]
                outs.append(acc[0:D_HEAD, :] / acc[D_HEAD:D_HEAD + 1, :])
            o_ref[0, :, hp * LANES:(hp + 1) * LANES] = (
                jnp.transpose(jnp.concatenate(outs, axis=0)).astype(o_ref.dtype))


def _fox_prompt(qT, k, vT, caug, *, tq):
    b, c, t = qT.shape
    n_q = t // tq
    pairs = [(i, j) for i in range(n_q) for j in range(i + 1)]
    qi = jnp.asarray([p[0] for p in pairs], jnp.int32)
    kj = jnp.asarray([p[1] for p in pairs], jnp.int32)
    grid_spec = pltpu.PrefetchScalarGridSpec(
        num_scalar_prefetch=2,
        grid=(b, len(pairs)),
        in_specs=[pl.BlockSpec((1, c, tq), lambda bi, s, qi, kj: (bi, 0, qi[s])),
                  pl.BlockSpec((1, tq, c), lambda bi, s, qi, kj: (bi, kj[s], 0)),
                  pl.BlockSpec((1, tq, LANES), lambda bi, s, qi, kj: (bi, kj[s], 0)),
                  pl.BlockSpec((1, c, tq), lambda bi, s, qi, kj: (bi, 0, kj[s]))],
        out_specs=pl.BlockSpec((1, tq, c), lambda bi, s, qi, kj: (bi, qi[s], 0)),
        scratch_shapes=[pltpu.VMEM((N_HEADS, 2 * LANES, tq), BF16), pltpu.VMEM((N_HEADS, 1, tq), F32),
                        pltpu.VMEM((N_HEADS, D_HEAD + ONES_ROWS, tq), F32),
                        pltpu.VMEM((tq, tq), F32), pltpu.VMEM((tq, tq), F32)])
    return pl.pallas_call(
        _fox_kernel,
        grid_spec=grid_spec,
        out_shape=jax.ShapeDtypeStruct((b, t, c), BF16),
        compiler_params=_cparams(("arbitrary", "arbitrary")),
        name="fox",
    )(qi, kj, qT, k, caug, vT)


def _fox_cache_kernel(q_ref, kn_ref, vn_ref, kpT_ref, vpT_ref, lfp_ref, lfn_ref, o_ref, cum_ref):
    t = q_ref.shape[1]
    p_len = kpT_ref.shape[3]
    c = q_ref.shape[2]
    n_rows = N_HEADS * t

    r = lax.broadcasted_iota(jnp.int32, (LANES, LANES), 0)
    cc = lax.broadcasted_iota(jnp.int32, (LANES, LANES), 1)
    upper = (r <= cc).astype(BF16)
    carry = jnp.zeros((N_HEADS, 1), F32)
    for i in range(p_len // LANES):
        cs = _dot_exact_rhs(lfp_ref[0, 0, :, i * LANES:(i + 1) * LANES], upper) + carry
        cum_ref[:, i * LANES:(i + 1) * LANES] = cs
        carry = cs[:, LANES - 1:LANES]
    cum_new = _dot_exact_rhs(lfn_ref[0], upper[:t, :t]) + carry

    def per_query_rows(cum):
        n = cum.shape[1]
        return jnp.broadcast_to(cum[:, None, :], (N_HEADS, t, n)).reshape(n_rows, n)

    q = q_ref[0]
    qrep = jnp.concatenate([q] * N_HEADS, axis=0)
    qr = lax.broadcasted_iota(jnp.int32, (n_rows, c), 0)
    ql = lax.broadcasted_iota(jnp.int32, (n_rows, c), 1)
    qbd = jnp.where(qr // t == ql // D_HEAD, qrep, jnp.zeros_like(qrep))

    s_past = _dot(qbd, kpT_ref[0, 0].astype(BF16)) - per_query_rows(cum_ref[...])
    s_new = _dot_nt(qbd, kn_ref[0]) - per_query_rows(cum_new)
    qry = lax.broadcasted_iota(jnp.int32, (n_rows, t), 0) % t
    key = lax.broadcasted_iota(jnp.int32, (n_rows, t), 1)
    s_new = jnp.where(key <= qry, s_new, -jnp.inf)

    m = jnp.maximum(jnp.max(s_past, axis=1, keepdims=True), jnp.max(s_new, axis=1, keepdims=True))
    p_past = jnp.exp(s_past - m)
    p_new = jnp.exp(s_new - m)
    inv = 1.0 / (jnp.sum(p_past, axis=1, keepdims=True) + jnp.sum(p_new, axis=1, keepdims=True))
    o_full = (_dot_nt((p_past * inv).astype(BF16), vpT_ref[0, 0].astype(BF16))
              + _dot((p_new * inv).astype(BF16), vn_ref[0]))
    for h in range(N_HEADS):
        o_ref[0, :, h * D_HEAD:(h + 1) * D_HEAD] = (
            o_full[h * t:(h + 1) * t, h * D_HEAD:(h + 1) * D_HEAD].astype(o_ref.dtype))


def _fox_cache(q, kn, vn, kpT, vpT, lf_pastT, lf_newT, layer):
    b, t, c = q.shape
    p_len = kpT.shape[3]
    assert p_len % LANES == 0 and t <= LANES
    new = lambda: pl.BlockSpec((1, t, c), lambda i: (i, 0, 0))
    past = lambda: pl.BlockSpec((1, 1, c, p_len), lambda i: (layer, i, 0, 0))
    return pl.pallas_call(
        _fox_cache_kernel,
        grid=(b,),
        in_specs=[new(), new(), new(), past(), past(),
                  pl.BlockSpec((1, 1, N_HEADS, p_len), lambda i: (layer, i, 0, 0)),
                  pl.BlockSpec((1, N_HEADS, t), lambda i: (i, 0, 0))],
        out_specs=new(),
        out_shape=jax.ShapeDtypeStruct((b, t, c), BF16),
        scratch_shapes=[pltpu.VMEM((N_HEADS, p_len), F32)],
        compiler_params=_cparams(("arbitrary",)),
        name="fox_cache",
    )(q, kn, vn, kpT, vpT, lf_pastT, lf_newT)


def _merge_kernel(x_ref, g_ref, wg_ref, u_ref, st_ref, wdw_ref, bdw_ref, lg_ref, lb_ref, b_ref, c_ref,
                  wa_ref, wb_ref, wc_ref, wo_ref, o_ref, xbuf, xsh, a_br):
    bb, tt, d = x_ref.shape
    tm = bb * tt
    _conv_context(st_ref, xbuf, tt, pl.program_id(1) == 0)
    x = x_ref[...].reshape(tm, d)
    h = _rms_rows(x, g_ref[...]).astype(BF16)
    block = 2 * LANES
    n_blocks = d // block

    def after(anchor, lhs):
        bits = lax.bitcast_convert_type(anchor, jnp.uint32)
        zero = lax.bitcast_convert_type((bits >> 16) >> 16, F32)
        head = lhs[0:2 * SUBLANES, :].astype(F32) + jnp.tile(zero, (2, lhs.shape[1] // LANES))
        return jnp.concatenate([head.astype(lhs.dtype), lhs[2 * SUBLANES:, :]], axis=0)

    prod_b = _dot(b_ref[...].reshape(tm, SEG), wb_ref[...])
    prod_c = _dot(c_ref[...].reshape(tm, SEG), wc_ref[...])
    conv = _conv_pieces(u_ref, wdw_ref, bdw_ref, lg_ref, lb_ref, a_br, xbuf, xsh)
    gates = [[None] * n_blocks for _ in range(3)]

    def gate_piece(n, i, lhs):
        gates[n][i] = _sigmoid(_dot(lhs, wg_ref[:, n * d + i * block:n * d + (i + 1) * block]))

    matmuls = [functools.partial(gate_piece, n, i) for i in range(n_blocks) for n in range(3)]
    done, anchor = 0, None
    for k, matmul in enumerate(matmuls):
        upto = max(done + 1, (k + 1) * len(conv) // len(matmuls))
        for piece in conv[done:upto]:
            out = piece()
            anchor = out if out is not None else anchor
        done = upto
        matmul(after(anchor, h) if anchor is not None else h)
    for piece in conv[done:]:
        piece()
    gate_a, gate_b, gate_c = (jnp.concatenate(g, axis=1) for g in gates)
    merged = gate_b * prod_b + gate_c * prod_c + gate_a * _dot(a_br[...], wa_ref[...])
    o_ref[...] = (x + _dot(merged.astype(BF16), wo_ref[...])).reshape(bb, tt, d)


def _merge(x, g, w_gates, u, conv_state, w_dw, b_dw, ln_g, ln_b, bm, c, wa, wb, wc, wo, *, bb, tt):
    b, t, d = x.shape
    width = w_dw.shape[0]
    st = jnp.pad(conv_state.astype(F32), ((0, 0), (HALO - (width - 1), 0), (0, 0)))
    const = lambda shape: pl.BlockSpec(shape, lambda i, j: (0,) * len(shape), pipeline_mode=pl.Buffered(1))
    act = lambda width: pl.BlockSpec((bb, tt, width), lambda i, j: (i, j, 0))
    return pl.pallas_call(
        _merge_kernel,
        grid=(b // bb, t // tt),
        in_specs=[act(d), const((1, d)), const(w_gates.shape), act(SEG),
                  pl.BlockSpec((bb, HALO, SEG), lambda i, j: (i, 0, 0)),
                  const(w_dw.shape), const((1, SEG)), const((1, SEG)), const((1, SEG)),
                  act(SEG), act(SEG),
                  const(wa.shape), const(wb.shape), const(wc.shape), const(wo.shape)],
        out_specs=act(d),
        out_shape=jax.ShapeDtypeStruct((b, t, d), F32),
        scratch_shapes=[pltpu.VMEM((bb, HALO + tt + SUBLANES, SEG), F32),
                        pltpu.VMEM((SUBLANES - 1, bb, HALO + tt, SEG), F32),
                        pltpu.VMEM((bb * tt, SEG), BF16)],
        compiler_params=_cparams(("arbitrary", "arbitrary")),
        name="merge",
    )(x, g, w_gates, u, st, w_dw, b_dw, ln_g, ln_b, bm, c, wa, wb, wc, wo)


def _erf(x):
    return lax.erf(x)


def _ffn_kernel(x_ref, g_ref, wup_ref, st_ref, wdw_ref, bdw_ref, wdn_ref, o_ref, buf_ref, abuf, *, n_chunks):
    bb, tt, d = x_ref.shape
    tm = bb * tt
    dff = wdn_ref.shape[0]
    cf = dff // n_chunks
    width = wdw_ref.shape[0]
    t = pl.program_id(1)
    x = x_ref[...].reshape(tm, d)
    h = _rms_rows(x, g_ref[...]).astype(BF16)

    @pl.when(t == 0)
    def _():
        abuf[:, 0:SUBLANES, :] = st_ref[...]

    @pl.when(t > 0)
    def _():
        abuf[:, 0:SUBLANES, :] = abuf[:, tt:tt + SUBLANES, :]

    out = x
    first = SUBLANES - (width - 1)
    for n in range(n_chunks):
        cs = slice(n * cf, (n + 1) * cf)
        abuf[:, SUBLANES:SUBLANES + tt, cs] = _dot(h, wup_ref[:, cs]).reshape(bb, tt, cf)
        gate = _dot(h, wup_ref[:, dff + n * cf:dff + (n + 1) * cf])
        conv = jnp.zeros((bb, tt, cf), F32) + bdw_ref[:, cs]
        for j in range(width):
            conv = conv + wdw_ref[j:j + 1, cs] * abuf[:, first + j:first + j + tt, cs]
        conv = conv.reshape(tm, cf)
        act = 0.5 * conv * (1.0 + _erf(conv * (2.0 ** -0.5))) * gate
        out = out + _dot(act.astype(BF16), wdn_ref[cs, :])
    o_ref[...] = out.reshape(bb, tt, d)
    buf_ref[...] = abuf[:, SUBLANES + tt - (width - 1):SUBLANES + tt, :]


def _ffn(x, g, w_up, state, w_dw, b_dw, w_down, *, bb, tt, n_chunks):
    b, t, d = x.shape
    dff = w_down.shape[0]
    width = w_dw.shape[0]
    st = jnp.pad(state.astype(F32), ((0, 0), (SUBLANES - (width - 1), 0), (0, 0)))
    const = lambda shape: pl.BlockSpec(shape, lambda i, j: (0,) * len(shape), pipeline_mode=pl.Buffered(1))
    return pl.pallas_call(
        functools.partial(_ffn_kernel, n_chunks=n_chunks),
        grid=(b // bb, t // tt),
        in_specs=[pl.BlockSpec((bb, tt, d), lambda i, j: (i, j, 0)),
                  const((1, d)), const(w_up.shape),
                  pl.BlockSpec((bb, SUBLANES, dff), lambda i, j: (i, 0, 0)),
                  const(w_dw.shape), const((1, dff)), const(w_down.shape)],
        out_specs=[pl.BlockSpec((bb, tt, d), lambda i, j: (i, j, 0)),
                   pl.BlockSpec((bb, width - 1, dff), lambda i, j: (i, 0, 0))],
        out_shape=[jax.ShapeDtypeStruct((b, t, d), F32),
                   jax.ShapeDtypeStruct((b, width - 1, dff), F32)],
        scratch_shapes=[pltpu.VMEM((bb, SUBLANES + tt, dff), F32)],
        compiler_params=_cparams(("arbitrary", "arbitrary")),
        name="ffn",
    )(x, g, w_up, st, w_dw, b_dw, w_down)


def _rope_tables(pos):
    half = D_HEAD // 2
    inv = jnp.exp(-math.log(ROPE_BASE) * jnp.arange(half, dtype=F32) / half)
    ang = pos.astype(F32)[:, None] * inv[None, :]
    cos, sin = jnp.cos(ang), jnp.sin(ang)
    cos_h = jnp.concatenate([cos, cos], axis=1)
    sin_h = jnp.concatenate([-sin, sin], axis=1)
    reps = LANES // D_HEAD
    return jnp.tile(cos_h, (1, reps)), jnp.tile(sin_h, (1, reps))


def _layer(x, pos, conv_state, ret_state, fox_past, ffn_state, lw, cfg, layer, depth, stacked):
    (g_mix, w_in, b_fgate, w_dw_conv, b_dw_conv, ln_conv_g, ln_conv_b, w_conv_out,
     gn_ret_g, w_ret_out, g_q_fox, g_k_fox, w_fox_out, w_out,
     g_ffn, w_ffn_up, w_ffn_dw, b_ffn_dw, w_ffn_down) = lw
    b, t, d = x.shape
    n_main = 9 * SEG
    row = lambda v: v.reshape(1, -1).astype(F32)

    w_main = w_in[:, :n_main].astype(BF16)
    w_ffT = jnp.transpose(w_in[:, n_main:n_main + N_HEADS]).astype(BF16)
    w_gates = w_in[:, n_main + N_HEADS:].astype(BF16)
    b_ff = b_fgate.reshape(N_HEADS, 1).astype(F32)
    cos_t, sin_t = _rope_tables(pos)
    head_id = np.arange(SEG) // D_HEAD
    bd = jnp.asarray((head_id[:, None] == head_id[None, :]) / D_HEAD, BF16)
    tile_head = lambda v: jnp.tile(v.astype(F32), N_HEADS).reshape(1, SEG)
    tile_head_t = lambda v: jnp.broadcast_to(jnp.tile(v.astype(F32), N_HEADS)[:, None], (SEG, LANES))

    if fox_past is None:
        wT = jnp.transpose(w_in[:, 6 * SEG:9 * SEG]).astype(BF16)
        (u, rq, rk, rv, sg, fqT, fkb, fvTb, fkT_all, fvT_all, lfT) = _inproj_seq(
            x, row(g_mix), w_main[:, :6 * SEG], w_ffT, b_ff, cos_t, sin_t,
            wT, tile_head_t(g_q_fox), tile_head_t(g_k_fox), stacked, layer, depth, tt=cfg["tt"])
        fox_state = (fkT_all, fvT_all, lfT)
        caug = _cumaug(jnp.transpose(lfT, (0, 2, 1)))
        c_br = _fox_prompt(fqT, fkb, fvTb, caug, tq=cfg["tq"])
    else:
        (u, rq, rk, rv, sg, fq, fk, fkb, fv, fvb), lfT = _inproj_rows(
            x, row(g_mix), w_main, w_ffT, b_ff, cos_t, sin_t,
            tile_head(g_q_fox), tile_head(g_k_fox), bd, bb=cfg["bb"], tt=cfg["tt"])
        fox_state = (fk.reshape(b, t, N_HEADS, D_HEAD), fv.reshape(b, t, N_HEADS, D_HEAD),
                     jnp.transpose(lfT, (0, 2, 1)))
        k_all, v_all, logf_all = fox_past
        p_len = k_all.shape[2]
        time_last = lambda a: jnp.transpose(a, (0, 1, 3, 4, 2)).reshape(depth, b, SEG, p_len)
        c_br = _fox_cache(fq, fkb, fvb, time_last(k_all), time_last(v_all),
                          jnp.transpose(logf_all.astype(F32), (0, 1, 3, 2)), lfT, layer)

    new_conv = u[:, t - (w_dw_conv.shape[0] - 1):, :]

    b_br, new_ret = _retention(rq, rk, rv, sg, ret_state.astype(F32), row(gn_ret_g), bd, rows=cfg["ret_rows"])

    x1 = _merge(x, row(g_mix), w_gates, u, conv_state, w_dw_conv.astype(F32), row(b_dw_conv),
                row(ln_conv_g), row(ln_conv_b), b_br, c_br, w_conv_out.astype(BF16), w_ret_out.astype(BF16),
                w_fox_out.astype(BF16), w_out.astype(BF16), bb=cfg["bb"], tt=cfg["tt_seq"])

    x2, new_ffn = _ffn(x1, row(g_ffn), w_ffn_up.astype(BF16), ffn_state, w_ffn_dw.astype(F32), row(b_ffn_dw),
                       w_ffn_down.astype(BF16), bb=cfg["bb"], tt=cfg["tt_seq"], n_chunks=cfg["ffn_chunks"])
    return x2, (new_conv, new_ret, new_ffn), fox_state


def _group_cfg(b, t):
    if t >= 512:
        return dict(bb=1, tt=512, tt_seq=512, ret_rows=256, tq=1024 if t % 1024 == 0 else 512, tm=512,
                    ffn_chunks=2)
    bb = max(1, min(b, 512 // t))
    return dict(bb=bb, tt=t, tt_seq=t, ret_rows=t, tq=t, tm=min(b * t, 512), ffn_chunks=2)


def kernel(x_prompt, x_sample, state_conv, state_ret, cache_fox_k, cache_fox_v, cache_fox_logf, state_ffn_conv, g_mix, w_in, b_fgate, w_dw_conv, b_dw_conv, ln_conv_g, ln_conv_b, w_conv_out, gn_ret_g, w_ret_out, g_q_fox, g_k_fox, w_fox_out, w_out, g_ffn, w_ffn_up, w_ffn_dw, b_ffn_dw, w_ffn_down):
    bp, sp, _ = x_prompt.shape
    bs, ts, _ = x_sample.shape
    depth = w_in.shape[0]
    p_len = cache_fox_k.shape[2]
    conv_w = w_dw_conv.shape[1]
    ffn_w = w_ffn_dw.shape[1]
    d_conv = w_dw_conv.shape[2]
    d_ff = w_ffn_down.shape[1]
    assert sp >= conv_w - 1 and ts >= conv_w - 1 and conv_w - 1 <= HALO and ffn_w - 1 <= SUBLANES
    assert d_conv == SEG and w_in.shape[2] == 9 * SEG + N_HEADS + 3 * x_prompt.shape[2]
    pos_p = jnp.arange(sp)
    pos_s = p_len + jnp.arange(ts)
    cfg_p = _group_cfg(bp, sp)
    cfg_s = _group_cfg(bs, ts)
    weights = (g_mix, w_in, b_fgate, w_dw_conv, b_dw_conv, ln_conv_g, ln_conv_b, w_conv_out,
               gn_ret_g, w_ret_out, g_q_fox, g_k_fox, w_fox_out, w_out,
               g_ffn, w_ffn_up, w_ffn_dw, b_ffn_dw, w_ffn_down)
    xp, xs = x_prompt, x_sample
    st_p, st_s, fox_s = [], [], []
    stacked = None
    for l in range(depth):
        lw = tuple(w[l] for w in weights)
        xp, sp_l, (fkT_all, fvT_all, lfT) = _layer(
            xp, pos_p,
            jnp.zeros((bp, conv_w - 1, d_conv), F32),
            jnp.zeros((bp, N_HEADS, D_HEAD, D_HEAD), F32),
            None,
            jnp.zeros((bp, ffn_w - 1, d_ff), F32), lw, cfg_p, l, depth, stacked)
        stacked = (fkT_all, fvT_all)
        xs, ss_l, fs_l = _layer(xs, pos_s, state_conv[l], state_ret[l],
                                (cache_fox_k, cache_fox_v, cache_fox_logf),
                                state_ffn_conv[l], lw, cfg_s, l, depth, None)
        st_p.append(sp_l + (lfT,))
        st_s.append(ss_l)
        fox_s.append(fs_l)
    stack = lambda sts, i: jnp.stack([s[i] for s in sts])
    heads_last = lambda a: jnp.transpose(a.reshape(depth, bp, N_HEADS, D_HEAD, sp), (0, 1, 4, 2, 3))
    return (xp, xs,
            stack(st_p, 0), stack(st_p, 1), heads_last(stacked[0]), heads_last(stacked[1]),
            jnp.transpose(stack(st_p, 3), (0, 1, 3, 2)), stack(st_p, 2),
            stack(st_s, 0), stack(st_s, 1), stack(fox_s, 0), stack(fox_s, 1), stack(fox_s, 2), stack(st_s, 2))
```
